```python
import math
import jax
import jax.numpy as jnp
from jax import lax
import numpy as np

D_MODEL = 4096
BATCH = 4
SEQ = 4096
DEPTH = 4
DEC_BATCH = 2
DEC_SEQ = 8192
PAST_LEN = 128

N_MIXERS = 3
RMS_EPS = 1e-6
SSD_EXPAND = 2
SSD_INNER = SSD_EXPAND * D_MODEL
SSD_HEAD_DIM = 64
SSD_HEADS = SSD_INNER // SSD_HEAD_DIM
SSD_GROUPS = 8
SSD_STATE = 128
SSD_CONV = 5
SSD_CHUNK = 128
SSD_CONV_DIM = SSD_INNER + 2 * SSD_GROUPS * SSD_STATE
SSD_IN_WIDTH = SSD_INNER + SSD_CONV_DIM + 2 * SSD_HEADS
HG_KDIM = 128
HG_HEADS = D_MODEL // HG_KDIM
HG_WIDTH = HG_HEADS * HG_KDIM
HG_CHUNK = 16
AT_HEAD_DIM = 128
AT_HEADS = D_MODEL // AT_HEAD_DIM
AT_KV_HEADS = 8
AT_WINDOW = 128
AT_BLOCK = 128
AT_Q_WIDTH = AT_HEADS * AT_HEAD_DIM
AT_KV_WIDTH = AT_KV_HEADS * AT_HEAD_DIM
ROPE_THETA = 10000.0
MEM_TOKENS = 256
CA_HEADS = 4
CA_HEAD_DIM = 128
CA_WIDTH = CA_HEADS * CA_HEAD_DIM
MOE_GROUPS = 4
MOE_EXPERTS_PER_GROUP = 4
MOE_EXPERTS = MOE_GROUPS * MOE_EXPERTS_PER_GROUP
MOE_TOP_K = 2
MOE_HIDDEN = D_MODEL // 4
MOE_BLOCK = 128
N_SSD_LAYERS = (DEPTH + 2) // 3
N_HG_LAYERS = (DEPTH + 1) // 3
N_AT_LAYERS = DEPTH // 3

kernel_name = 'hybrid_bidir_ssd_hgrn2_swa_hmoe'


def _rms_norm(x, g):
    xf = x.astype(jnp.float32)
    y = xf * lax.rsqrt(jnp.mean(xf * xf, axis=-1, keepdims=True) + RMS_EPS)
    return (y * g).astype(x.dtype)


def _segsum(a):
    t = a.shape[-1]
    cs = jnp.cumsum(a, axis=-1)
    diff = cs[..., :, None] - cs[..., None, :]
    mask = jnp.tril(jnp.ones((t, t), dtype=bool))
    return jnp.where(mask, diff, -jnp.inf)


def _centred_depthwise_conv(u, w, bias):
    width, c = w.shape
    pad = width // 2
    out = lax.conv_general_dilated(u, w[:, None, :].astype(u.dtype), window_strides=(1,),
                                   padding=((pad, pad),), dimension_numbers=('NWC', 'WIO', 'NWC'),
                                   feature_group_count=c)
    return out + bias.astype(u.dtype)


def _ssd_scan(xs, dt, a, bm, cm):
    b, s, h, p = xs.shape
    g, n = bm.shape[2], bm.shape[3]
    r = h // g
    c = s // SSD_CHUNK
    l = SSD_CHUNK
    xd = (xs * dt[..., None]).reshape(b, c, l, g, r, p)
    la = (dt * a).reshape(b, c, l, g, r).transpose(0, 3, 4, 1, 2)
    bc = bm.reshape(b, c, l, g, n)
    cc = cm.reshape(b, c, l, g, n)
    acum = jnp.cumsum(la, axis=-1)
    cb = jnp.einsum('bclgn,bcsgn->bgcls', cc, bc)
    w = cb[:, :, None] * jnp.exp(_segsum(la))
    y_diag = jnp.einsum('bgrcls,bcsgrp->bclgrp', w, xd)
    decay_to_end = jnp.exp(acum[..., -1:] - acum).transpose(0, 3, 4, 1, 2)
    states = jnp.einsum('bclgn,bclgrp->bcgrpn', bc, xd * decay_to_end[..., None])
    chunk_tot = jnp.pad(acum[..., -1], ((0, 0), (0, 0), (0, 0), (1, 0)))
    decay_chunk = jnp.exp(_segsum(chunk_tot))
    states0 = jnp.concatenate([jnp.zeros_like(states[:, :1]), states], axis=1)
    carried = jnp.einsum('bgrzc,bcgrpn->bzgrpn', decay_chunk, states0)[:, :-1]
    decay_in = jnp.exp(acum).transpose(0, 3, 4, 1, 2)
    y_off = jnp.einsum('bclgn,bcgrpn->bclgrp', cc, carried) * decay_in[..., None]
    return (y_diag + y_off).reshape(b, s, h, p)


def _ssd_mixer(x, w_in, conv_w, conv_b, dt_bias, a_log, d_skip, norm_g, w_out):
    b, s, _ = x.shape
    proj = x @ w_in
    z = proj[..., :SSD_INNER]
    xbc = proj[..., SSD_INNER:SSD_INNER + SSD_CONV_DIM]
    dt_raw = proj[..., SSD_INNER + SSD_CONV_DIM:]
    xbc = jax.nn.silu(_centred_depthwise_conv(xbc, conv_w, conv_b)).astype(jnp.float32)
    gn = SSD_GROUPS * SSD_STATE
    xs = xbc[..., :SSD_INNER].reshape(b, s, SSD_HEADS, SSD_HEAD_DIM)
    bm = xbc[..., SSD_INNER:SSD_INNER + gn].reshape(b, s, SSD_GROUPS, SSD_STATE)
    cm = xbc[..., SSD_INNER + gn:].reshape(b, s, SSD_GROUPS, SSD_STATE)
    dt = jax.nn.softplus(dt_raw.astype(jnp.float32).reshape(b, s, 2, SSD_HEADS) + dt_bias)
    a = -jnp.exp(a_log.astype(jnp.float32))
    y_f = _ssd_scan(xs, dt[:, :, 0], a[0], bm, cm)
    y_b = _ssd_scan(xs[:, ::-1], dt[:, ::-1, 1], a[1], bm[:, ::-1], cm[:, ::-1])[:, ::-1]
    y = y_f + y_b + d_skip[:, None] * xs
    y = y.reshape(b, s, SSD_INNER) * jax.nn.silu(z.astype(jnp.float32))
    yg = y.reshape(b, s, SSD_GROUPS, SSD_INNER // SSD_GROUPS)
    yg = yg * lax.rsqrt(jnp.mean(yg * yg, axis=-1, keepdims=True) + RMS_EPS)
    y = yg.reshape(b, s, SSD_INNER) * norm_g
    return y.astype(x.dtype) @ w_out


def _gla_chunk_scan(q, k, v, log_f):
    b, s, h, dk = q.shape
    dv = v.shape[-1]
    c = s // HG_CHUNK

    def to_chunks(t):
        return t.reshape(b, c, HG_CHUNK, h, t.shape[-1]).transpose(1, 0, 2, 3, 4)

    mask = jnp.tril(jnp.ones((HG_CHUNK, HG_CHUNK), dtype=bool))

    def step(state, inp):
        qc, kc, vc, gc = inp
        bcum = jnp.cumsum(gc, axis=1)
        b_last = bcum[:, -1:]
        q_in = qc * jnp.exp(bcum)
        k_in = kc * jnp.exp(-bcum)
        att = jnp.where(mask, jnp.einsum('blhk,bshk->bhls', q_in, k_in), 0.0)
        o = jnp.einsum('bhls,bshv->blhv', att, vc) + jnp.einsum('blhk,bhkv->blhv', q_in, state)
        k_out = kc * jnp.exp(b_last - bcum)
        state = jnp.exp(b_last[:, 0])[..., None] * state + jnp.einsum('bshk,bshv->bhkv', k_out, vc)
        return state, o

    s0 = jnp.zeros((b, h, dk, dv), jnp.float32)
    _, o = lax.scan(step, s0, (to_chunks(q), to_chunks(k), to_chunks(v), to_chunks(log_f)))
    return o.transpose(1, 0, 2, 3, 4).reshape(b, s, h, dv)


def _hgrn2_mixer(x, w_in, lower_bound, norm_g, w_out):
    b, s, _ = x.shape
    proj = (x @ w_in).astype(jnp.float32).reshape(b, s, 5, HG_HEADS, HG_KDIM)
    q = jax.nn.silu(proj[:, :, 0])
    v = proj[:, :, 3]
    gate = proj[:, :, 4]
    lb = lower_bound.astype(jnp.float32).reshape(HG_HEADS, HG_KDIM)
    f_fwd = lb + (1.0 - lb) * jax.nn.sigmoid(proj[:, :, 1])
    f_bwd = lb + (1.0 - lb) * jax.nn.sigmoid(proj[:, :, 2])
    o_f = _gla_chunk_scan(q, 1.0 - f_fwd, v, jnp.log(f_fwd))
    o_b = _gla_chunk_scan(q[:, ::-1], (1.0 - f_bwd)[:, ::-1], v[:, ::-1], jnp.log(f_bwd)[:, ::-1])[:, ::-1]
    o = o_f + o_b
    o = o * lax.rsqrt(jnp.mean(o * o, axis=-1, keepdims=True) + RMS_EPS) * norm_g
    o = o * jax.nn.silu(gate)
    return o.reshape(b, s, HG_WIDTH).astype(x.dtype) @ w_out


def _rope(x, pos):
    d = x.shape[-1]
    half = d // 2
    inv = ROPE_THETA ** (-jnp.arange(half, dtype=jnp.float32) / half)
    ang = pos[:, None] * inv[None, :]
    cos = jnp.cos(ang)[None, :, None, :]
    sin = jnp.sin(ang)[None, :, None, :]
    xf = x.astype(jnp.float32)
    x1, x2 = xf[..., :half], xf[..., half:]
    return jnp.concatenate([x1 * cos - x2 * sin, x2 * cos + x1 * sin], axis=-1).astype(x.dtype)


def _banded_attention(q, k, v, sink):
    b, s, hq, d = q.shape
    hkv = k.shape[2]
    r = hq // hkv
    nb = s // AT_BLOCK
    qb = q.reshape(b, nb, AT_BLOCK, hkv, r, d)
    pad = ((0, 0), (AT_BLOCK, AT_BLOCK), (0, 0), (0, 0))
    kp = jnp.pad(k, pad).reshape(b, nb + 2, AT_BLOCK, hkv, d)
    vp = jnp.pad(v, pad).reshape(b, nb + 2, AT_BLOCK, hkv, d)
    kb = jnp.concatenate([kp[:, :-2], kp[:, 1:-1], kp[:, 2:]], axis=2)
    vb = jnp.concatenate([vp[:, :-2], vp[:, 1:-1], vp[:, 2:]], axis=2)
    scores = jnp.einsum('bnqgrd,bnkgd->bngrqk', qb, kb).astype(jnp.float32) * (AT_HEAD_DIM ** -0.5)
    blk = jnp.arange(nb)[:, None, None] * AT_BLOCK
    qpos = blk + jnp.arange(AT_BLOCK)[None, :, None]
    kpos = blk - AT_BLOCK + jnp.arange(3 * AT_BLOCK)[None, None, :]
    valid = (jnp.abs(qpos - kpos) <= AT_WINDOW) & (kpos >= 0) & (kpos < s)
    scores = jnp.where(valid[None, :, None, None], scores, -jnp.inf)
    sink_l = sink.astype(jnp.float32).reshape(hkv, r)[None, None, :, :, None, None]
    m = jnp.maximum(jnp.max(scores, axis=-1, keepdims=True), sink_l)
    p = jnp.exp(scores - m)
    denom = jnp.sum(p, axis=-1, keepdims=True) + jnp.exp(sink_l - m)
    out = jnp.einsum('bngrqk,bnkgd->bnqgrd', (p / denom).astype(v.dtype), vb)
    return out.reshape(b, s, hq, d)


def _window_gqa_mixer(x, w_qkv, sink, w_out):
    b, s, _ = x.shape
    qkv = x @ w_qkv
    q = qkv[..., :AT_Q_WIDTH].reshape(b, s, AT_HEADS, AT_HEAD_DIM)
    k = qkv[..., AT_Q_WIDTH:AT_Q_WIDTH + AT_KV_WIDTH].reshape(b, s, AT_KV_HEADS, AT_HEAD_DIM)
    v = qkv[..., AT_Q_WIDTH + AT_KV_WIDTH:].reshape(b, s, AT_KV_HEADS, AT_HEAD_DIM)
    pos = jnp.arange(s, dtype=jnp.float32)
    o = _banded_attention(_rope(q, pos), _rope(k, pos), v, sink)
    return o.reshape(b, s, AT_Q_WIDTH) @ w_out


def _cross_attention(x, mem, w_q, w_kv, w_out):
    b, s, _ = x.shape
    m = mem.shape[1]
    q = (x @ w_q).reshape(b, s, CA_HEADS, CA_HEAD_DIM)
    kv = (mem @ w_kv).reshape(b, m, 2, CA_HEADS, CA_HEAD_DIM)
    scores = jnp.einsum('bshd,bmhd->bhsm', q, kv[:, :, 0]).astype(jnp.float32) * (CA_HEAD_DIM ** -0.5)
    p = jax.nn.softmax(scores, axis=-1)
    o = jnp.einsum('bhsm,bmhd->bshd', p.astype(x.dtype), kv[:, :, 1])
    return o.reshape(b, s, CA_WIDTH) @ w_out


def _routed_experts(xf, experts, gates, w_gate, w_up, w_down):
    n, d = xf.shape
    nk = n * MOE_TOP_K
    e_flat = experts.reshape(nk)
    tok_flat = jnp.arange(nk) // MOE_TOP_K
    order = jnp.argsort(e_flat)
    e_sorted = e_flat[order]
    tok_sorted = tok_flat[order]
    gate_sorted = gates.reshape(nk)[order]
    counts = jnp.zeros((MOE_EXPERTS,), jnp.int32).at[e_flat].add(1)
    padded = (counts + MOE_BLOCK - 1) // MOE_BLOCK * MOE_BLOCK
    start = jnp.cumsum(counts) - counts
    pend = jnp.cumsum(padded)
    pstart = pend - padded
    dest = pstart[e_sorted] + (jnp.arange(nk) - start[e_sorted])
    n_blocks = nk // MOE_BLOCK + MOE_EXPERTS
    rows = n_blocks * MOE_BLOCK
    buf = jnp.zeros((rows, d), xf.dtype).at[dest].set(xf[tok_sorted])
    blk_expert = jnp.minimum(jnp.searchsorted(pend, jnp.arange(n_blocks) * MOE_BLOCK, side='right'),
                             MOE_EXPERTS - 1)

    def block_mlp(args):
        xb, e = args
        hdn = jax.nn.silu(xb @ w_gate[e]) * (xb @ w_up[e])
        return hdn @ w_down[e]

    out = lax.map(block_mlp, (buf.reshape(n_blocks, MOE_BLOCK, d), blk_expert)).reshape(rows, d)
    y = jnp.zeros((n, d), out.dtype).at[tok_sorted].add(out[dest] * gate_sorted[:, None].astype(out.dtype))
    return y.astype(xf.dtype)


def _hier_moe(x, w_group, b_group, w_expert, b_expert, w_gate, w_up, w_down):
    bsz, s, d = x.shape
    n = bsz * s
    xf = x.reshape(n, d)
    g_logits = (xf @ w_group).astype(jnp.float32) + b_group
    g_prob = jax.nn.softmax(g_logits, axis=-1)
    g_sel = jnp.argmax(g_logits, axis=-1)
    g_weight = jnp.take_along_axis(g_prob, g_sel[:, None], axis=-1)
    e_logits = ((xf @ w_expert).astype(jnp.float32) + b_expert).reshape(n, MOE_GROUPS, MOE_EXPERTS_PER_GROUP)
    e_logits = jnp.take_along_axis(e_logits, g_sel[:, None, None], axis=1)[:, 0]
    top_val, top_idx = lax.top_k(e_logits, MOE_TOP_K)
    gates = g_weight * jax.nn.softmax(top_val, axis=-1)
    experts = g_sel[:, None] * MOE_EXPERTS_PER_GROUP + top_idx
    return _routed_experts(xf, experts, gates, w_gate, w_up, w_down).reshape(bsz, s, d)


def _trunk(x, mem, p):
    lb_soft = jax.nn.softmax(p['hg_lb'].astype(jnp.float32), axis=0)
    lower_bounds = jnp.cumsum(lb_soft, axis=0) - lb_soft[0]
    h = x
    for i in range(DEPTH):
        kind, j = i % N_MIXERS, i // N_MIXERS
        hn = _rms_norm(h, p['norm_mix'][i])
        if kind == 0:
            mix = _ssd_mixer(hn, p['ssd_w_in'][j], p['ssd_conv_w'][j], p['ssd_conv_b'][j], p['ssd_dt_bias'][j],
                             p['ssd_a_log'][j], p['ssd_d'][j], p['ssd_norm'][j], p['ssd_w_out'][j])
        elif kind == 1:
            mix = _hgrn2_mixer(hn, p['hg_w_in'][j], lower_bounds[i], p['hg_norm'][j], p['hg_w_out'][j])
        else:
            mix = _window_gqa_mixer(hn, p['at_w_qkv'][j], p['at_sink'][j], p['at_w_out'][j])
        h = h + mix
        h = h + _cross_attention(_rms_norm(h, p['norm_cross'][i]), _rms_norm(mem, p['norm_mem'][i]),
                                 p['ca_w_q'][i], p['ca_w_kv'][i], p['ca_w_out'][i])
        h = h + _hier_moe(_rms_norm(h, p['norm_moe'][i]), p['moe_w_group'][i], p['moe_b_group'][i],
                          p['moe_w_expert'][i], p['moe_b_expert'][i], p['moe_w_gate'][i],
                          p['moe_w_up'][i], p['moe_w_down'][i])
    return _rms_norm(h, p['norm_final'])


def setup_inputs(seed: int = 0) -> dict:
    key = jax.random.key(seed)
    ks = iter(jax.random.split(key, 48))
    f32 = jnp.float32

    def normal(shape, scale):
        return scale * jax.random.normal(next(ks), shape, f32)

    def gain(shape):
        return 1.0 + normal(shape, 0.02)

    D = D_MODEL
    out_gain = 0.5
    x_prompt = normal((BATCH, SEQ, D), 1.0)
    x_sample = normal((DEC_BATCH, DEC_SEQ, D), 1.0)
    mem_prompt = normal((BATCH, MEM_TOKENS, D), 1.0)
    mem_sample = normal((DEC_BATCH, MEM_TOKENS, D), 1.0)
    norm_mix = gain((DEPTH, D))
    norm_cross = gain((DEPTH, D))
    norm_mem = gain((DEPTH, D))
    norm_moe = gain((DEPTH, D))
    norm_final = gain((D,))
    ssd_w_in = normal((N_SSD_LAYERS, D, SSD_IN_WIDTH), D ** -0.5)
    ssd_conv_w = normal((N_SSD_LAYERS, SSD_CONV, SSD_CONV_DIM), SSD_CONV ** -0.5)
    ssd_conv_b = normal((N_SSD_LAYERS, SSD_CONV_DIM), 0.01)
    dt = jnp.exp(jax.random.uniform(next(ks), (N_SSD_LAYERS, 2, SSD_HEADS), f32, math.log(1e-3), math.log(1e-1)))
    ssd_dt_bias = dt + jnp.log(-jnp.expm1(-dt))
    ssd_a_log = jnp.log(jax.random.uniform(next(ks), (N_SSD_LAYERS, 2, SSD_HEADS), f32, 1.0, 16.0))
    ssd_d = 1.0 + normal((N_SSD_LAYERS, SSD_HEADS), 0.1)
    ssd_norm = gain((N_SSD_LAYERS, SSD_INNER))
    ssd_w_out = normal((N_SSD_LAYERS, SSD_INNER, D), out_gain * SSD_INNER ** -0.5)
    hg_w_in = normal((N_HG_LAYERS, D, 5 * HG_WIDTH), D ** -0.5)
    hg_lb = normal((DEPTH, HG_WIDTH), 0.1)
    hg_norm = gain((N_HG_LAYERS, HG_KDIM))
    hg_w_out = normal((N_HG_LAYERS, HG_WIDTH, D), out_gain * HG_WIDTH ** -0.5)
    at_w_qkv = normal((N_AT_LAYERS, D, AT_Q_WIDTH + 2 * AT_KV_WIDTH), D ** -0.5)
    at_sink = normal((N_AT_LAYERS, AT_HEADS), 0.5)
    at_w_out = normal((N_AT_LAYERS, AT_Q_WIDTH, D), out_gain * AT_Q_WIDTH ** -0.5)
    ca_w_q = normal((DEPTH, D, CA_WIDTH), D ** -0.5)
    ca_w_kv = normal((DEPTH, D, 2 * CA_WIDTH), D ** -0.5)
    ca_w_out = normal((DEPTH, CA_WIDTH, D), out_gain * CA_WIDTH ** -0.5)
    moe_w_group = normal((DEPTH, D, MOE_GROUPS), D ** -0.5)
    moe_b_group = normal((DEPTH, MOE_GROUPS), 0.01)
    moe_w_expert = normal((DEPTH, D, MOE_EXPERTS), D ** -0.5)
    moe_b_expert = normal((DEPTH, MOE_EXPERTS), 0.01)
    moe_w_gate = normal((DEPTH, MOE_EXPERTS, D, MOE_HIDDEN), D ** -0.5)
    moe_w_up = normal((DEPTH, MOE_EXPERTS, D, MOE_HIDDEN), D ** -0.5)
    moe_w_down = normal((DEPTH, MOE_EXPERTS, MOE_HIDDEN, D), out_gain * MOE_HIDDEN ** -0.5)
    return {'x_prompt': x_prompt, 'x_sample': x_sample, 'mem_prompt': mem_prompt, 'mem_sample': mem_sample,
            'norm_mix': norm_mix, 'norm_cross': norm_cross, 'norm_mem': norm_mem, 'norm_moe': norm_moe,
            'norm_final': norm_final,
            'ssd_w_in': ssd_w_in, 'ssd_conv_w': ssd_conv_w, 'ssd_conv_b': ssd_conv_b, 'ssd_dt_bias': ssd_dt_bias,
            'ssd_a_log': ssd_a_log, 'ssd_d': ssd_d, 'ssd_norm': ssd_norm, 'ssd_w_out': ssd_w_out,
            'hg_w_in': hg_w_in, 'hg_lb': hg_lb, 'hg_norm': hg_norm, 'hg_w_out': hg_w_out,
            'at_w_qkv': at_w_qkv, 'at_sink': at_sink, 'at_w_out': at_w_out,
            'ca_w_q': ca_w_q, 'ca_w_kv': ca_w_kv, 'ca_w_out': ca_w_out,
            'moe_w_group': moe_w_group, 'moe_b_group': moe_b_group, 'moe_w_expert': moe_w_expert,
            'moe_b_expert': moe_b_expert, 'moe_w_gate': moe_w_gate, 'moe_w_up': moe_w_up, 'moe_w_down': moe_w_down}


def reference(x_prompt, x_sample, mem_prompt, mem_sample,
              norm_mix, norm_cross, norm_mem, norm_moe, norm_final,
              ssd_w_in, ssd_conv_w, ssd_conv_b, ssd_dt_bias, ssd_a_log, ssd_d, ssd_norm, ssd_w_out,
              hg_w_in, hg_lb, hg_norm, hg_w_out,
              at_w_qkv, at_sink, at_w_out,
              ca_w_q, ca_w_kv, ca_w_out,
              moe_w_group, moe_b_group, moe_w_expert, moe_b_expert, moe_w_gate, moe_w_up, moe_w_down):
    params = {'norm_mix': norm_mix, 'norm_cross': norm_cross, 'norm_mem': norm_mem, 'norm_moe': norm_moe,
              'norm_final': norm_final,
              'ssd_w_in': ssd_w_in, 'ssd_conv_w': ssd_conv_w, 'ssd_conv_b': ssd_conv_b, 'ssd_dt_bias': ssd_dt_bias,
              'ssd_a_log': ssd_a_log, 'ssd_d': ssd_d, 'ssd_norm': ssd_norm, 'ssd_w_out': ssd_w_out,
              'hg_w_in': hg_w_in, 'hg_lb': hg_lb, 'hg_norm': hg_norm, 'hg_w_out': hg_w_out,
              'at_w_qkv': at_w_qkv, 'at_sink': at_sink, 'at_w_out': at_w_out,
              'ca_w_q': ca_w_q, 'ca_w_kv': ca_w_kv, 'ca_w_out': ca_w_out,
              'moe_w_group': moe_w_group, 'moe_b_group': moe_b_group, 'moe_w_expert': moe_w_expert,
              'moe_b_expert': moe_b_expert, 'moe_w_gate': moe_w_gate, 'moe_w_up': moe_w_up,
              'moe_w_down': moe_w_down}
    y_prompt = _trunk(x_prompt, mem_prompt, params)
    y_sample = _trunk(x_sample, mem_sample, params)
    return (y_prompt, y_sample)
```

```python
import functools
import math

import jax
import jax.numpy as jnp
from jax import lax
from jax.experimental import pallas as pl
from jax.experimental.pallas import tpu as pltpu

F32 = jnp.float32
BF16 = jnp.bfloat16

RMS_EPS = 1e-6
ROPE_THETA = 10000.0
SSD_HEAD_DIM = 64
SSD_GROUPS = 8
SSD_STATE = 128
SSD_CHUNK = 128
HG_KDIM = 128
HG_CHUNK = 16
AT_HEAD_DIM = 128
AT_KV_HEADS = 8
AT_BLOCK = 128
CA_HEADS = 4
CA_HEAD_DIM = 128
MOE_GROUPS = 4
MOE_EXPERTS_PER_GROUP = 4
MOE_EXPERTS = MOE_GROUPS * MOE_EXPERTS_PER_GROUP
MOE_TOP_K = 2
N_MIXERS = 3

V7X_VMEM_LIMIT_BYTES = 56 * 1024 * 1024
LANES = 128


def _params(*sem):
    return pltpu.CompilerParams(dimension_semantics=sem, vmem_limit_bytes=V7X_VMEM_LIMIT_BYTES)


def _tile(n, pref):
    t = min(n, pref)
    while n % t:
        t //= 2
    return t


def _split_dot(m01, x):
    hi = x.astype(BF16)
    lo = (x - hi.astype(F32)).astype(BF16)
    return (jnp.dot(m01, hi, preferred_element_type=F32) + jnp.dot(m01, lo, preferred_element_type=F32))


def _silu(x):
    return x * jax.nn.sigmoid(x)


def _rmsnorm_body(x_ref, g_ref, o_ref):
    x = x_ref[...].astype(F32)
    ms = jnp.mean(x * x, axis=-1, keepdims=True)
    o_ref[...] = (x * lax.rsqrt(ms + RMS_EPS) * g_ref[...]).astype(o_ref.dtype)


def _rmsnorm(x, g, out_dtype):
    m, d = x.shape
    tm = _tile(m, 256)
    return pl.pallas_call(
        _rmsnorm_body,
        grid=(m // tm,),
        in_specs=[pl.BlockSpec((tm, d), lambda i: (i, 0)), pl.BlockSpec((1, d), lambda i: (0, 0))],
        out_specs=pl.BlockSpec((tm, d), lambda i: (i, 0)),
        out_shape=jax.ShapeDtypeStruct((m, d), out_dtype),
        compiler_params=_params("parallel"),
        name="rmsnorm",
    )(x, g.reshape(1, d).astype(F32))


def _matmul_body(*refs, nk, has_res):
    if has_res:
        x_ref, w_ref, r_ref, o_ref = refs[:4]
        scratch = refs[4:]
    else:
        x_ref, w_ref, o_ref = refs[:3]
        r_ref = None
        scratch = refs[3:]
    if nk == 1:
        acc = jnp.dot(x_ref[...], w_ref[...], preferred_element_type=F32)
        if has_res:
            acc = acc + r_ref[...]
        o_ref[...] = acc.astype(o_ref.dtype)
        return
    acc_ref, = scratch
    k = pl.program_id(2)

    @pl.when(k == 0)
    def _():
        acc_ref[...] = jnp.zeros_like(acc_ref)

    acc_ref[...] += jnp.dot(x_ref[...], w_ref[...], preferred_element_type=F32)

    @pl.when(k == nk - 1)
    def _():
        acc = acc_ref[...]
        if has_res:
            acc = acc + r_ref[...]
        o_ref[...] = acc.astype(o_ref.dtype)


def _matmul(x, w, out_dtype, residual=None):
    m, k = x.shape
    n = w.shape[1]
    tm = _tile(m, 1024)
    tn = _tile(n, 1024)
    tk = _tile(k, 4096 if residual is None else 2048)
    nk = k // tk
    has_res = residual is not None
    in_specs = [pl.BlockSpec((tm, tk), lambda i, j, kk: (i, kk)),
                pl.BlockSpec((tk, tn), lambda i, j, kk: (kk, j))]
    args = [x, w]
    if has_res:
        in_specs.append(pl.BlockSpec((tm, tn), lambda i, j, kk: (i, j)))
        args.append(residual)
    return pl.pallas_call(
        functools.partial(_matmul_body, nk=nk, has_res=has_res),
        grid=(m // tm, n // tn, nk),
        in_specs=in_specs,
        out_specs=pl.BlockSpec((tm, tn), lambda i, j, kk: (i, j)),
        out_shape=jax.ShapeDtypeStruct((m, n), out_dtype),
        scratch_shapes=[pltpu.VMEM((tm, tn), F32)] if nk > 1 else [],
        compiler_params=_params("parallel", "parallel", "arbitrary"),
        name="matmul",
    )(*args)


CONV_HALO = 16


def _ssd_conv_body(x_ref, w_ref, b_ref, o_ref, *, seq, tr, width):
    pad = width // 2
    w = w_ref[...]
    bias = b_ref[...]
    n_chunks = seq // tr

    def chunk(ci, carry):
        r0 = pl.multiple_of(ci * tr, tr)
        main = x_ref[0, pl.ds(r0, tr), :].astype(F32)
        p0 = pl.multiple_of(jnp.maximum(r0 - CONV_HALO, 0), CONV_HALO)
        n0 = pl.multiple_of(jnp.minimum(r0 + tr, seq - CONV_HALO), CONV_HALO)
        prev = x_ref[0, pl.ds(p0, CONV_HALO), :].astype(F32)
        nxt = x_ref[0, pl.ds(n0, CONV_HALO), :].astype(F32)
        prev = jnp.where(ci > 0, prev, 0.0)
        nxt = jnp.where(ci < n_chunks - 1, nxt, 0.0)
        ext = jnp.concatenate([prev, main, nxt], axis=0)
        acc = jnp.zeros_like(main) + bias
        for t in range(width):
            lo = CONV_HALO + t - pad
            acc = acc + ext[lo:lo + tr, :] * w[t:t + 1, :]
        o_ref[0, pl.ds(r0, tr), :] = _silu(acc).astype(o_ref.dtype)
        return carry

    lax.fori_loop(0, n_chunks, chunk, 0)


def _ssd_conv(xbc, conv_w, conv_b):
    b, s, c = xbc.shape
    width = conv_w.shape[0]
    tc = _tile(c, 256)
    tr = _tile(s, 256)
    return pl.pallas_call(
        functools.partial(_ssd_conv_body, seq=s, tr=tr, width=width),
        grid=(b, c // tc),
        in_specs=[pl.BlockSpec((1, s, tc), lambda i, j: (i, 0, j)),
                  pl.BlockSpec((width, tc), lambda i, j: (0, j)),
                  pl.BlockSpec((1, tc), lambda i, j: (0, j))],
        out_specs=pl.BlockSpec((1, s, tc), lambda i, j: (i, 0, j)),
        out_shape=jax.ShapeDtypeStruct((b, s, c), BF16),
        compiler_params=_params("parallel", "parallel"),
        name="ssd_conv",
    )(xbc, conv_w.astype(F32), conv_b.reshape(1, c).astype(F32))


def _ssd_scan_body(x_ref, b_ref, c_ref, dt_ref, a_ref, y_ref, state_ref, *, reverse, heads, hdim):
    l = x_ref.shape[1]
    width = heads * hdim

    @pl.when(pl.program_id(2) == 0)
    def _():
        state_ref[...] = jnp.zeros_like(state_ref)

    row = lax.broadcasted_iota(jnp.int32, (l, l), 0)
    col = lax.broadcasted_iota(jnp.int32, (l, l), 1)
    causal = (col >= row) if reverse else (col <= row)
    tri = causal.astype(BF16)
    eye = (row == col).astype(BF16)
    hh = lax.broadcasted_iota(jnp.int32, (heads, width), 0)
    cc = lax.broadcasted_iota(jnp.int32, (heads, width), 1)
    expand = (cc // hdim == hh).astype(BF16)

    dt_t = dt_ref[0]
    la_t = dt_t * a_ref[...]

    def nt_split(m01, x_t):
        hi = x_t.astype(BF16)
        lo = (x_t - hi.astype(F32)).astype(BF16)
        dn = (((1,), (1,)), ((), ()))
        return (lax.dot_general(m01, hi, dn, preferred_element_type=F32)
                + lax.dot_general(m01, lo, dn, preferred_element_type=F32))

    cum = nt_split(tri, la_t)
    dt_col = nt_split(eye, dt_t)
    hi = la_t.astype(BF16)
    lo = (la_t - hi.astype(F32)).astype(BF16)
    dn = (((1,), (1,)), ((), ()))
    cum_t = (lax.dot_general(hi, tri, dn, preferred_element_type=F32)
             + lax.dot_general(lo, tri, dn, preferred_element_type=F32))

    cum_x = _split_dot_rhs01(cum, expand)
    dt_x = _split_dot_rhs01(dt_col, expand)
    last = 0 if reverse else l - 1
    tot_x = cum_x[last:last + 1, :]

    x = x_ref[0].astype(F32)
    xd = x * dt_x
    bm = b_ref[0]
    cm = c_ref[0]
    g = lax.dot_general(cm, bm, dn, preferred_element_type=F32)
    xd_b = xd.astype(BF16)
    ys = []
    for h in range(heads):
        d = cum[:, h:h + 1] - cum_t[h:h + 1, :]
        wgt = (g * jnp.exp(jnp.where(causal, d, -1e30))).astype(BF16)
        ys.append(jnp.dot(wgt, xd_b[:, h * hdim:(h + 1) * hdim], preferred_element_type=F32))
    y = jnp.concatenate(ys, axis=1)
    state = state_ref[...]
    y = y + jnp.dot(cm, state.astype(BF16), preferred_element_type=F32) * jnp.exp(cum_x)
    y_ref[0] = y.astype(y_ref.dtype)
    xdd = (xd * jnp.exp(tot_x - cum_x)).astype(BF16)
    upd = lax.dot_general(bm, xdd, (((0,), (0,)), ((), ())), preferred_element_type=F32)
    state_ref[...] = state * jnp.exp(tot_x) + upd


def _split_dot_rhs01(x, m01):
    hi = x.astype(BF16)
    lo = (x - hi.astype(F32)).astype(BF16)
    return (jnp.dot(hi, m01, preferred_element_type=F32) + jnp.dot(lo, m01, preferred_element_type=F32))


def _ssd_scan(xbc, inner, dt_t, a, *, reverse):
    b, s, _ = xbc.shape
    groups = SSD_GROUPS
    n = SSD_STATE
    width = inner // groups
    b_off = inner // n
    c_off = b_off + groups
    heads = width // SSD_HEAD_DIM
    l = SSD_CHUNK
    nc = s // l

    def cidx(c):
        return nc - 1 - c if reverse else c

    return pl.pallas_call(
        functools.partial(_ssd_scan_body, reverse=reverse, heads=heads, hdim=SSD_HEAD_DIM),
        grid=(b, groups, nc),
        in_specs=[pl.BlockSpec((1, l, width), lambda i, g, c: (i, cidx(c), g)),
                  pl.BlockSpec((1, l, n), lambda i, g, c: (i, cidx(c), b_off + g)),
                  pl.BlockSpec((1, l, n), lambda i, g, c: (i, cidx(c), c_off + g)),
                  pl.BlockSpec((1, heads, l), lambda i, g, c: (i, g, cidx(c))),
                  pl.BlockSpec((heads, 1), lambda i, g, c: (g, 0))],
        out_specs=pl.BlockSpec((1, l, width), lambda i, g, c: (i, cidx(c), g)),
        out_shape=jax.ShapeDtypeStruct((b, s, inner), BF16),
        scratch_shapes=[pltpu.VMEM((n, width), F32)],
        compiler_params=_params("parallel", "parallel", "arbitrary"),
        name="ssd_scan_bwd" if reverse else "ssd_scan_fwd",
    )(xbc, xbc, xbc, dt_t, a.reshape(-1, 1).astype(F32))


def _ssd_gate_body(yf_ref, yb_ref, x_ref, z_ref, d_ref, g_ref, o_ref):
    x = x_ref[...].astype(F32)
    y = yf_ref[...].astype(F32) + yb_ref[...].astype(F32) + d_ref[...] * x
    y = y * _silu(z_ref[...].astype(F32))
    ms = jnp.mean(y * y, axis=-1, keepdims=True)
    o_ref[...] = (y * lax.rsqrt(ms + RMS_EPS) * g_ref[...]).astype(o_ref.dtype)


def _ssd_gate(yf, yb, xbc, z, d_x, norm_g):
    m, inner = yf.shape
    width = inner // SSD_GROUPS
    tm = _tile(m, 512)
    blk = pl.BlockSpec((tm, width), lambda i, g: (i, g))
    vec = pl.BlockSpec((1, width), lambda i, g: (0, g))
    return pl.pallas_call(
        _ssd_gate_body,
        grid=(m // tm, SSD_GROUPS),
        in_specs=[blk, blk, blk, blk, vec, vec],
        out_specs=blk,
        out_shape=jax.ShapeDtypeStruct((m, inner), BF16),
        compiler_params=_params("parallel", "parallel"),
        name="ssd_gate",
    )(yf, yb, xbc, z, d_x.reshape(1, inner).astype(F32), norm_g.reshape(1, inner).astype(F32))


def _ssd_mixer(h, hn, bsz, seq, w_in, conv_w, conv_b, dt_bias, a_log, d_skip, norm_g, w_out):
    m, d = hn.shape
    inner = w_out.shape[0]
    heads = inner // SSD_HEAD_DIM
    gn = SSD_GROUPS * SSD_STATE
    conv_dim = inner + 2 * gn
    w_in = w_in.astype(BF16)
    z = _matmul(hn, w_in[:, :inner], BF16)
    xbc = _matmul(hn, w_in[:, inner:inner + conv_dim], BF16)
    dt_raw = _matmul(hn, w_in[:, inner + conv_dim:], F32)
    xbc = _ssd_conv(xbc.reshape(bsz, seq, conv_dim), conv_w, conv_b)
    dt = jax.nn.softplus(dt_raw.reshape(bsz, seq, 2, heads) + dt_bias.astype(F32))
    dt_t = jnp.transpose(dt, (2, 0, 3, 1))
    a = -jnp.exp(a_log.astype(F32))
    y_f = _ssd_scan(xbc, inner, dt_t[0], a[0], reverse=False)
    y_b = _ssd_scan(xbc, inner, dt_t[1], a[1], reverse=True)
    d_x = jnp.repeat(d_skip.astype(F32), SSD_HEAD_DIM)
    y = _ssd_gate(y_f.reshape(m, inner), y_b.reshape(m, inner), xbc.reshape(m, conv_dim), z, d_x, norm_g)
    return _matmul(y, w_out.astype(BF16), F32, residual=h)


def _hg_scan_body(q_ref, f_ref, v_ref, lb_ref, o_ref, state_ref, *, reverse, chunk):
    r = q_ref.shape[1]
    nch = r // chunk

    @pl.when(pl.program_id(2) == 0)
    def _():
        state_ref[...] = jnp.zeros_like(state_ref)

    row = lax.broadcasted_iota(jnp.int32, (r, r), 0)
    col = lax.broadcasted_iota(jnp.int32, (r, r), 1)
    same = (row // chunk) == (col // chunk)
    causal = same & ((col >= row) if reverse else (col <= row))
    m_cum = causal.astype(BF16)
    m_all = same.astype(BF16)

    lb = lb_ref[0]
    f = lb + (1.0 - lb) * jax.nn.sigmoid(f_ref[0])
    logf = jnp.log(f)
    kk = 1.0 - f
    q = _silu(q_ref[0].astype(F32))
    v = v_ref[0]
    cum = _split_dot(m_cum, logf)
    tot = _split_dot(m_all, logf)
    q_in = (q * jnp.exp(cum)).astype(BF16)
    k_in = (kk * jnp.exp(-cum)).astype(BF16)
    k_out = (kk * jnp.exp(tot - cum)).astype(BF16)
    e_tot = jnp.exp(tot)
    att = lax.dot_general(q_in, k_in, (((1,), (1,)), ((), ())), preferred_element_type=F32)
    att = jnp.where(causal, att, 0.0).astype(BF16)
    o_intra = jnp.dot(att, v, preferred_element_type=F32)

    order = range(nch - 1, -1, -1) if reverse else range(nch)
    st = state_ref[...]
    outs = [None] * nch
    for c in order:
        sl = slice(c * chunk, (c + 1) * chunk)
        outs[c] = lax.dot_general(q_in[sl], st.astype(BF16), (((1,), (1,)), ((), ())),
                                  preferred_element_type=F32)
        upd = lax.dot_general(v[sl], k_out[sl], (((0,), (0,)), ((), ())), preferred_element_type=F32)
        st = st * e_tot[c * chunk:c * chunk + 1, :] + upd
    state_ref[...] = st
    o_ref[0] = (o_intra + jnp.concatenate(outs, axis=0)).astype(o_ref.dtype)


def _hg_scan(q, f, v, lb, *, reverse):
    b, s, wdt = q.shape
    heads = wdt // HG_KDIM
    r = _tile(s, 256)
    nb = s // r

    def ridx(c):
        return nb - 1 - c if reverse else c

    blk = pl.BlockSpec((1, r, HG_KDIM), lambda i, h, c: (i, ridx(c), h))
    return pl.pallas_call(
        functools.partial(_hg_scan_body, reverse=reverse, chunk=HG_CHUNK),
        grid=(b, heads, nb),
        in_specs=[blk, blk, blk, pl.BlockSpec((1, 1, HG_KDIM), lambda i, h, c: (h, 0, 0))],
        out_specs=blk,
        out_shape=jax.ShapeDtypeStruct((b, s, wdt), BF16),
        scratch_shapes=[pltpu.VMEM((HG_KDIM, HG_KDIM), F32)],
        compiler_params=_params("parallel", "parallel", "arbitrary"),
        name="hg_scan_bwd" if reverse else "hg_scan_fwd",
    )(q, f, v, lb)


def _hg_gate_body(of_ref, ob_ref, gate_ref, g_ref, o_ref):
    o = of_ref[...].astype(F32) + ob_ref[...].astype(F32)
    ms = jnp.mean(o * o, axis=-1, keepdims=True)
    o = o * lax.rsqrt(ms + RMS_EPS) * g_ref[...]
    o_ref[...] = (o * _silu(gate_ref[...].astype(F32))).astype(o_ref.dtype)


def _hg_gate(o_f, o_b, gate, norm_g):
    m, wdt = o_f.shape
    heads = wdt // HG_KDIM
    tm = _tile(m, 1024)
    blk = pl.BlockSpec((tm, HG_KDIM), lambda i, h: (i, h))
    return pl.pallas_call(
        _hg_gate_body,
        grid=(m // tm, heads),
        in_specs=[blk, blk, blk, pl.BlockSpec((1, HG_KDIM), lambda i, h: (0, 0))],
        out_specs=blk,
        out_shape=jax.ShapeDtypeStruct((m, wdt), BF16),
        compiler_params=_params("parallel", "parallel"),
        name="hg_gate",
    )(o_f, o_b, gate, norm_g.reshape(1, HG_KDIM).astype(F32))


def _hgrn2_mixer(h, hn, bsz, seq, w_in, lower_bound, norm_g, w_out):
    m, d = hn.shape
    wdt = w_out.shape[0]
    heads = wdt // HG_KDIM
    w_in = w_in.astype(BF16)
    q = _matmul(hn, w_in[:, :wdt], BF16).reshape(bsz, seq, wdt)
    f_fwd = _matmul(hn, w_in[:, wdt:2 * wdt], F32).reshape(bsz, seq, wdt)
    f_bwd = _matmul(hn, w_in[:, 2 * wdt:3 * wdt], F32).reshape(bsz, seq, wdt)
    v = _matmul(hn, w_in[:, 3 * wdt:4 * wdt], BF16).reshape(bsz, seq, wdt)
    gate = _matmul(hn, w_in[:, 4 * wdt:], BF16)
    lb = lower_bound.astype(F32).reshape(heads, 1, HG_KDIM)
    o_f = _hg_scan(q, f_fwd, v, lb, reverse=False)
    o_b = _hg_scan(q, f_bwd, v, lb, reverse=True)
    o = _hg_gate(o_f.reshape(m, wdt), o_b.reshape(m, wdt), gate, norm_g)
    return _matmul(o, w_out.astype(BF16), F32, residual=h)


def _rope_body(x_ref, cos_ref, sin_ref, o_ref, *, hd):
    x = x_ref[0].astype(F32)
    rot = pltpu.roll(x, hd // 2, axis=1)
    o_ref[0] = (x * cos_ref[...] + rot * sin_ref[...]).astype(o_ref.dtype)


def _rope(x, cos2, sin2, n_heads):
    b, s, _ = x.shape
    hd = AT_HEAD_DIM
    ts = _tile(s, 512)
    blk = pl.BlockSpec((1, ts, hd), lambda i, j, hh: (i, j, hh))
    tab = pl.BlockSpec((ts, hd), lambda i, j, hh: (j, 0))
    return pl.pallas_call(
        functools.partial(_rope_body, hd=hd),
        grid=(b, s // ts, n_heads),
        in_specs=[blk, tab, tab],
        out_specs=blk,
        out_shape=jax.ShapeDtypeStruct((b, s, n_heads * hd), BF16),
        compiler_params=_params("parallel", "parallel", "parallel"),
        name="rope",
    )(x, cos2, sin2)


def _win_attn_body(q_ref, kp_ref, kc_ref, kn_ref, vp_ref, vc_ref, vn_ref, sink_ref, o_ref, *, rep, hd, nb):
    n = pl.program_id(2)
    blk = q_ref.shape[1]
    kcat = jnp.concatenate([kp_ref[0], kc_ref[0], kn_ref[0]], axis=0)
    vcat = jnp.concatenate([vp_ref[0], vc_ref[0], vn_ref[0]], axis=0)
    qi = lax.broadcasted_iota(jnp.int32, (blk, 3 * blk), 0) + blk
    kj = lax.broadcasted_iota(jnp.int32, (blk, 3 * blk), 1)
    valid = jnp.abs(qi - kj) <= blk
    valid = valid & ((kj >= blk) | (n > 0)) & ((kj < 2 * blk) | (n < nb - 1))
    scale = hd ** -0.5
    outs = []
    for r in range(rep):
        q = q_ref[0, :, r * hd:(r + 1) * hd]
        s = lax.dot_general(q, kcat, (((1,), (1,)), ((), ())), preferred_element_type=F32) * scale
        s = jnp.where(valid, s, -jnp.inf)
        sink = sink_ref[0, :, r:r + 1]
        mx = jnp.maximum(jnp.max(s, axis=-1, keepdims=True), sink)
        p = jnp.exp(s - mx)
        denom = jnp.sum(p, axis=-1, keepdims=True) + jnp.exp(sink - mx)
        outs.append(jnp.dot((p / denom).astype(BF16), vcat, preferred_element_type=F32))
    o_ref[0] = jnp.concatenate(outs, axis=1).astype(o_ref.dtype)


def _win_attn(qk, qkv, qw, sink):
    b, s, _ = qk.shape
    hd = AT_HEAD_DIM
    hkv = AT_KV_HEADS
    rep = qw // hd // hkv
    k_off = qw // hd
    v_off = k_off + hkv
    blk = AT_BLOCK
    nb = s // blk
    qspec = pl.BlockSpec((1, blk, rep * hd), lambda i, g, n: (i, n, g))

    def kv(off, col):
        return pl.BlockSpec((1, blk, hd), lambda i, g, n: (i, jnp.clip(n + off, 0, nb - 1), col + g))

    return pl.pallas_call(
        functools.partial(_win_attn_body, rep=rep, hd=hd, nb=nb),
        grid=(b, hkv, nb),
        in_specs=[qspec, kv(-1, k_off), kv(0, k_off), kv(1, k_off), kv(-1, v_off), kv(0, v_off), kv(1, v_off),
                  pl.BlockSpec((1, 1, rep), lambda i, g, n: (g, 0, 0))],
        out_specs=qspec,
        out_shape=jax.ShapeDtypeStruct((b, s, qw), BF16),
        compiler_params=_params("parallel", "parallel", "parallel"),
        name="win_attn",
    )(qk, qk, qk, qk, qkv, qkv, qkv, sink.astype(F32).reshape(hkv, 1, rep))


def _window_gqa_mixer(h, hn, bsz, seq, w_qkv, sink, w_out):
    m, d = hn.shape
    qw = w_out.shape[0]
    kvw = AT_KV_HEADS * AT_HEAD_DIM
    qkv = _matmul(hn, w_qkv.astype(BF16), BF16).reshape(bsz, seq, qw + 2 * kvw)
    half = AT_HEAD_DIM // 2
    inv = ROPE_THETA ** (-jnp.arange(half, dtype=F32) / half)
    ang = jnp.arange(seq, dtype=F32)[:, None] * inv[None, :]
    cos2 = jnp.concatenate([jnp.cos(ang), jnp.cos(ang)], axis=1)
    sin2 = jnp.concatenate([-jnp.sin(ang), jnp.sin(ang)], axis=1)
    qk = _rope(qkv, cos2, sin2, (qw + kvw) // AT_HEAD_DIM)
    o = _win_attn(qk, qkv, qw, sink)
    return _matmul(o.reshape(m, qw), w_out.astype(BF16), F32, residual=h)


def _cross_attn_body(h_ref, g_ref, wq_ref, kv_ref, wo_ref, o_ref, *, heads, hd):
    hres = h_ref[0]
    ms = jnp.mean(hres * hres, axis=-1, keepdims=True)
    xn = (hres * lax.rsqrt(ms + RMS_EPS) * g_ref[...]).astype(BF16)
    q = jnp.dot(xn, wq_ref[...], preferred_element_type=F32).astype(BF16)
    kv = kv_ref[0]
    scale = hd ** -0.5
    outs = []
    for a in range(heads):
        k = kv[:, a * hd:(a + 1) * hd]
        v = kv[:, (heads + a) * hd:(heads + a + 1) * hd]
        s = lax.dot_general(q[:, a * hd:(a + 1) * hd], k, (((1,), (1,)), ((), ())),
                            preferred_element_type=F32) * scale
        mx = jnp.max(s, axis=-1, keepdims=True)
        p = jnp.exp(s - mx)
        p = p / jnp.sum(p, axis=-1, keepdims=True)
        outs.append(jnp.dot(p.astype(BF16), v, preferred_element_type=F32))
    o = jnp.concatenate(outs, axis=1).astype(BF16)
    o_ref[0] = hres + jnp.dot(o, wo_ref[...], preferred_element_type=F32)


def _cross_attention(h, mem, bsz, seq, norm_x, norm_m, w_q, w_kv, w_out):
    m, d = h.shape
    mt = mem.shape[1]
    cw = w_q.shape[1]
    memn = _rmsnorm(mem.reshape(bsz * mt, d), norm_m, BF16)
    kv = _matmul(memn, w_kv.astype(BF16), BF16).reshape(bsz, mt, 2 * cw)
    tm = _tile(seq, 256)
    out = pl.pallas_call(
        functools.partial(_cross_attn_body, heads=CA_HEADS, hd=CA_HEAD_DIM),
        grid=(bsz, seq // tm),
        in_specs=[pl.BlockSpec((1, tm, d), lambda i, j: (i, j, 0)),
                  pl.BlockSpec((1, d), lambda i, j: (0, 0)),
                  pl.BlockSpec((d, cw), lambda i, j: (0, 0)),
                  pl.BlockSpec((1, mt, 2 * cw), lambda i, j: (i, 0, 0)),
                  pl.BlockSpec((cw, d), lambda i, j: (0, 0))],
        out_specs=pl.BlockSpec((1, tm, d), lambda i, j: (i, j, 0)),
        out_shape=jax.ShapeDtypeStruct((bsz, seq, d), F32),
        compiler_params=_params("parallel", "parallel"),
        name="cross_attn",
    )(h.reshape(bsz, seq, d), norm_x.reshape(1, d).astype(F32), w_q.astype(BF16), kv, w_out.astype(BF16))
    return out.reshape(m, d)


MOE_ROW_BLOCK = 256
MOE_HIDDEN_TILE = 512
MOE_TOKEN_TILE = 256


def _moe_norm_logits_body(h_ref, g_ref, wr_ref, xn_ref, lg_ref):
    x = h_ref[...]
    ms = jnp.mean(x * x, axis=-1, keepdims=True)
    xn = x * lax.rsqrt(ms + RMS_EPS) * g_ref[...]
    xn_ref[...] = xn
    lg_ref[...] = jnp.dot(xn, wr_ref[...], preferred_element_type=F32, precision=lax.Precision.HIGHEST)


def _moe_norm_logits(h, g, w_route):
    m, d = h.shape
    tm = _tile(m, 256)
    return pl.pallas_call(
        _moe_norm_logits_body,
        grid=(m // tm,),
        in_specs=[pl.BlockSpec((tm, d), lambda i: (i, 0)),
                  pl.BlockSpec((1, d), lambda i: (0, 0)),
                  pl.BlockSpec((d, LANES), lambda i: (0, 0))],
        out_specs=[pl.BlockSpec((tm, d), lambda i: (i, 0)), pl.BlockSpec((tm, LANES), lambda i: (i, 0))],
        out_shape=[jax.ShapeDtypeStruct((m, d), F32), jax.ShapeDtypeStruct((m, LANES), F32)],
        compiler_params=_params("parallel"),
        name="moe_norm_logits",
    )(h, g.reshape(1, d).astype(F32), w_route)


def _row_copy(src_hbm, dst_ref, sem, src_row, dst_row):
    return pltpu.make_async_copy(src_hbm.at[pl.ds(src_row, 1)], dst_ref.at[pl.ds(dst_row, 1)], sem)


def _gather_rows_body(idx_ref, src_hbm, o_ref, sem, *, tm):
    base = pl.program_id(0) * tm

    def start(r, carry):
        _row_copy(src_hbm, o_ref, sem, idx_ref[base + r], r).start()
        return carry

    def wait(r, carry):
        _row_copy(src_hbm, o_ref, sem, 0, r).wait()
        return carry

    lax.fori_loop(0, tm, start, 0)
    lax.fori_loop(0, tm, wait, 0)


def _gather_rows(src, idx):
    rows = idx.shape[0]
    d = src.shape[1]
    tm = _tile(rows, 256)
    return pl.pallas_call(
        functools.partial(_gather_rows_body, tm=tm),
        grid_spec=pltpu.PrefetchScalarGridSpec(
            num_scalar_prefetch=1,
            grid=(rows // tm,),
            in_specs=[pl.BlockSpec(memory_space=pl.ANY)],
            out_specs=pl.BlockSpec((tm, d), lambda i, idx_ref: (i, 0)),
            scratch_shapes=[pltpu.SemaphoreType.DMA(())]),
        out_shape=jax.ShapeDtypeStruct((rows, d), src.dtype),
        compiler_params=_params("arbitrary"),
        name="moe_gather",
    )(idx, src)


def _experts_body(be_ref, nused_ref, x_ref, wg_ref, wu_ref, wd_ref, o_ref):
    i = pl.program_id(0)
    j = pl.program_id(1)
    used = i < nused_ref[0]

    @pl.when(jnp.logical_and(j == 0, jnp.logical_not(used)))
    def _():
        o_ref[...] = jnp.zeros_like(o_ref)

    @pl.when(used)
    def _():
        x = x_ref[...].astype(BF16)
        hid = (_silu(jnp.dot(x, wg_ref[0], preferred_element_type=F32))
               * jnp.dot(x, wu_ref[0], preferred_element_type=F32)).astype(BF16)
        part = jnp.dot(hid, wd_ref[0], preferred_element_type=F32)

        @pl.when(j == 0)
        def _():
            o_ref[...] = part

        @pl.when(j > 0)
        def _():
            o_ref[...] += part


def _experts(buf, blk_expert, n_used, w_gate, w_up, w_down):
    rows, d = buf.shape
    hidden = w_gate.shape[2]
    tm = MOE_ROW_BLOCK
    th = _tile(hidden, MOE_HIDDEN_TILE)
    nj = hidden // th
    nblk = rows // tm

    def xmap(i, j, be, nu):
        return (jnp.minimum(i, nu[0] - 1), 0)

    return pl.pallas_call(
        _experts_body,
        grid_spec=pltpu.PrefetchScalarGridSpec(
            num_scalar_prefetch=2,
            grid=(nblk, nj),
            in_specs=[pl.BlockSpec((tm, d), xmap),
                      pl.BlockSpec((1, d, th), lambda i, j, be, nu: (be[i], 0, j)),
                      pl.BlockSpec((1, d, th), lambda i, j, be, nu: (be[i], 0, j)),
                      pl.BlockSpec((1, th, d), lambda i, j, be, nu: (be[i], j, 0))],
            out_specs=pl.BlockSpec((tm, d), lambda i, j, be, nu: (i, 0))),
        out_shape=jax.ShapeDtypeStruct((rows, d), F32),
        compiler_params=_params("arbitrary", "arbitrary"),
        name="moe_experts",
    )(blk_expert, n_used, buf, w_gate, w_up, w_down)


def _combine_body(d0_ref, d1_ref, src_hbm, h_ref, gate_ref, o_ref, rows0, rows1, sem, *, tm):
    base = pl.program_id(0) * tm

    def start(r, carry):
        _row_copy(src_hbm, rows0, sem.at[0], d0_ref[base + r], r).start()
        _row_copy(src_hbm, rows1, sem.at[1], d1_ref[base + r], r).start()
        return carry

    def wait(r, carry):
        _row_copy(src_hbm, rows0, sem.at[0], 0, r).wait()
        _row_copy(src_hbm, rows1, sem.at[1], 0, r).wait()
        return carry

    lax.fori_loop(0, tm, start, 0)
    lax.fori_loop(0, tm, wait, 0)
    g = gate_ref[...]
    o_ref[...] = h_ref[...] + rows0[...] * g[:, 0:1] + rows1[...] * g[:, 1:2]


def _combine(h, out_buf, dest, gates):
    m, d = h.shape
    tm = _tile(m, MOE_TOKEN_TILE)
    return pl.pallas_call(
        functools.partial(_combine_body, tm=tm),
        grid_spec=pltpu.PrefetchScalarGridSpec(
            num_scalar_prefetch=2,
            grid=(m // tm,),
            in_specs=[pl.BlockSpec(memory_space=pl.ANY),
                      pl.BlockSpec((tm, d), lambda i, a, b: (i, 0)),
                      pl.BlockSpec((tm, MOE_TOP_K), lambda i, a, b: (i, 0))],
            out_specs=pl.BlockSpec((tm, d), lambda i, a, b: (i, 0)),
            scratch_shapes=[pltpu.VMEM((tm, d), F32), pltpu.VMEM((tm, d), F32),
                            pltpu.SemaphoreType.DMA((2,))]),
        out_shape=jax.ShapeDtypeStruct((m, d), F32),
        compiler_params=_params("arbitrary"),
        name="moe_combine",
    )(dest[:, 0], dest[:, 1], out_buf, h, gates)


def _hier_moe(h, norm_g, w_group, b_group, w_expert, b_expert, w_gate, w_up, w_down):
    n, d = h.shape
    w_route = jnp.concatenate([w_group, w_expert], axis=1).astype(F32)
    w_route = jnp.pad(w_route, ((0, 0), (0, LANES - w_route.shape[1])))
    xn, logits = _moe_norm_logits(h, norm_g, w_route)
    g_logits = logits[:, :MOE_GROUPS] + b_group.astype(F32)
    g_prob = jax.nn.softmax(g_logits, axis=-1)
    g_sel = jnp.argmax(g_logits, axis=-1)
    g_weight = jnp.take_along_axis(g_prob, g_sel[:, None], axis=-1)
    e_logits = (logits[:, MOE_GROUPS:MOE_GROUPS + MOE_EXPERTS] + b_expert.astype(F32)
                ).reshape(n, MOE_GROUPS, MOE_EXPERTS_PER_GROUP)
    e_logits = jnp.take_along_axis(e_logits, g_sel[:, None, None], axis=1)[:, 0]
    top_val, top_idx = lax.top_k(e_logits, MOE_TOP_K)
    gates = g_weight * jax.nn.softmax(top_val, axis=-1)
    experts = (g_sel[:, None] * MOE_EXPERTS_PER_GROUP + top_idx).astype(jnp.int32)

    nk = n * MOE_TOP_K
    tm = MOE_ROW_BLOCK
    e_flat = experts.reshape(nk)
    onehot = (e_flat[:, None] == jnp.arange(MOE_EXPERTS, dtype=jnp.int32)[None, :]).astype(jnp.int32)
    rank = jnp.sum((jnp.cumsum(onehot, axis=0) - onehot) * onehot, axis=1)
    counts = jnp.sum(onehot, axis=0)
    padded = (counts + tm - 1) // tm * tm
    pend = jnp.cumsum(padded)
    pstart = pend - padded
    dest = (pstart[e_flat] + rank).astype(jnp.int32)
    n_blocks = -(-nk // tm) + MOE_EXPERTS
    rows = n_blocks * tm
    src = jnp.zeros((rows,), jnp.int32).at[dest].set(jnp.arange(nk, dtype=jnp.int32) // MOE_TOP_K)
    blk_expert = jnp.minimum(jnp.searchsorted(pend, jnp.arange(n_blocks, dtype=jnp.int32) * tm, side='right'),
                             MOE_EXPERTS - 1).astype(jnp.int32)
    n_used = (pend[-1] // tm).astype(jnp.int32).reshape(1)

    buf = _gather_rows(xn, src)
    out_buf = _experts(buf, blk_expert, n_used, w_gate.astype(BF16), w_up.astype(BF16), w_down.astype(BF16))
    return _combine(h, out_buf, dest.reshape(n, MOE_TOP_K), gates.astype(F32))


def _trunk(x, mem, p):
    bsz, seq, d = x.shape
    m = bsz * seq
    depth = p['norm_mix'].shape[0]
    lb_soft = jax.nn.softmax(p['hg_lb'].astype(F32), axis=0)
    lower_bounds = jnp.cumsum(lb_soft, axis=0) - lb_soft[0]
    h = x.reshape(m, d)
    for i in range(depth):
        kind, j = i % N_MIXERS, i // N_MIXERS
        hn = _rmsnorm(h, p['norm_mix'][i], BF16)
        if kind == 0:
            h = _ssd_mixer(h, hn, bsz, seq, p['ssd_w_in'][j], p['ssd_conv_w'][j], p['ssd_conv_b'][j],
                           p['ssd_dt_bias'][j], p['ssd_a_log'][j], p['ssd_d'][j], p['ssd_norm'][j],
                           p['ssd_w_out'][j])
        elif kind == 1:
            h = _hgrn2_mixer(h, hn, bsz, seq, p['hg_w_in'][j], lower_bounds[i], p['hg_norm'][j], p['hg_w_out'][j])
        else:
            h = _window_gqa_mixer(h, hn, bsz, seq, p['at_w_qkv'][j], p['at_sink'][j], p['at_w_out'][j])
        h = _cross_attention(h, mem, bsz, seq, p['norm_cross'][i], p['norm_mem'][i],
                             p['ca_w_q'][i], p['ca_w_kv'][i], p['ca_w_out'][i])
        h = _hier_moe(h, p['norm_moe'][i], p['moe_w_group'][i], p['moe_b_group'][i], p['moe_w_expert'][i],
                      p['moe_b_expert'][i], p['moe_w_gate'][i], p['moe_w_up'][i], p['moe_w_down'][i])
    return _rmsnorm(h, p['norm_final'], F32).reshape(bsz, seq, d)


def kernel(x_prompt, x_sample, mem_prompt, mem_sample, norm_mix, norm_cross, norm_mem, norm_moe, norm_final,
           ssd_w_in, ssd_conv_w, ssd_conv_b, ssd_dt_bias, ssd_a_log, ssd_d, ssd_norm, ssd_w_out,
           hg_w_in, hg_lb, hg_norm, hg_w_out, at_w_qkv, at_sink, at_w_out, ca_w_q, ca_w_kv, ca_w_out,
           moe_w_group, moe_b_group, moe_w_expert, moe_b_expert, moe_w_gate, moe_w_up, moe_w_down):
    p = dict(norm_mix=norm_mix, norm_cross=norm_cross, norm_mem=norm_mem, norm_moe=norm_moe,
             norm_final=norm_final, ssd_w_in=ssd_w_in, ssd_conv_w=ssd_conv_w, ssd_conv_b=ssd_conv_b,
             ssd_dt_bias=ssd_dt_bias, ssd_a_log=ssd_a_log, ssd_d=ssd_d, ssd_norm=ssd_norm, ssd_w_out=ssd_w_out,
             hg_w_in=hg_w_in, hg_lb=hg_lb, hg_norm=hg_norm, hg_w_out=hg_w_out,
             at_w_qkv=at_w_qkv, at_sink=at_sink, at_w_out=at_w_out,
             ca_w_q=ca_w_q, ca_w_kv=ca_w_kv, ca_w_out=ca_w_out,
             moe_w_group=moe_w_group, moe_b_group=moe_b_group, moe_w_expert=moe_w_expert,
             moe_b_expert=moe_b_expert, moe_w_gate=moe_w_gate, moe_w_up=moe_w_up, moe_w_down=moe_w_down)
    return (_trunk(x_prompt, mem_prompt, p), _trunk(x_sample, mem_sample, p))
```

```python
import functools
import math

import jax
import jax.numpy as jnp
from jax import lax
from jax.experimental import pallas as pl
from jax.experimental.pallas import tpu as pltpu

F32 = jnp.float32
BF16 = jnp.bfloat16

RMS_EPS = 1e-6
ROPE_THETA = 10000.0
SSD_HEAD_DIM = 64
SSD_GROUPS = 8
SSD_STATE = 128
SSD_CHUNK = 128
HG_KDIM = 128
HG_CHUNK = 16
AT_HEAD_DIM = 128
AT_KV_HEADS = 8
AT_BLOCK = 128
CA_HEADS = 4
CA_HEAD_DIM = 128
MOE_GROUPS = 4
MOE_EXPERTS_PER_GROUP = 4
MOE_EXPERTS = MOE_GROUPS * MOE_EXPERTS_PER_GROUP
MOE_TOP_K = 2
N_MIXERS = 3

V7X_VMEM_LIMIT_BYTES = 56 * 1024 * 1024
LANES = 128


def _params(*sem):
    return pltpu.CompilerParams(dimension_semantics=sem, vmem_limit_bytes=V7X_VMEM_LIMIT_BYTES)


def _tile(n, pref):
    t = min(n, pref)
    while n % t:
        t //= 2
    return t


def _split_dot(m01, x):
    hi = x.astype(BF16)
    lo = (x - hi.astype(F32)).astype(BF16)
    return (jnp.dot(m01, hi, preferred_element_type=F32) + jnp.dot(m01, lo, preferred_element_type=F32))


def _silu(x):
    return x * jax.nn.sigmoid(x)


def _rmsnorm_body(x_ref, g_ref, o_ref):
    x = x_ref[...].astype(F32)
    ms = jnp.mean(x * x, axis=-1, keepdims=True)
    o_ref[...] = (x * lax.rsqrt(ms + RMS_EPS) * g_ref[...]).astype(o_ref.dtype)


def _rmsnorm(x, g, out_dtype):
    m, d = x.shape
    tm = _tile(m, 256)
    return pl.pallas_call(
        _rmsnorm_body,
        grid=(m // tm,),
        in_specs=[pl.BlockSpec((tm, d), lambda i: (i, 0)), pl.BlockSpec((1, d), lambda i: (0, 0))],
        out_specs=pl.BlockSpec((tm, d), lambda i: (i, 0)),
        out_shape=jax.ShapeDtypeStruct((m, d), out_dtype),
        compiler_params=_params("parallel"),
        name="rmsnorm",
    )(x, g.reshape(1, d).astype(F32))


def _matmul_body(*refs, nk, has_res):
    if has_res:
        x_ref, w_ref, r_ref, o_ref = refs[:4]
        scratch = refs[4:]
    else:
        x_ref, w_ref, o_ref = refs[:3]
        r_ref = None
        scratch = refs[3:]
    if nk == 1:
        acc = jnp.dot(x_ref[...], w_ref[...], preferred_element_type=F32)
        if has_res:
            acc = acc + r_ref[...]
        o_ref[...] = acc.astype(o_ref.dtype)
        return
    acc_ref, = scratch
    k = pl.program_id(2)

    @pl.when(k == 0)
    def _():
        acc_ref[...] = jnp.zeros_like(acc_ref)

    acc_ref[...] += jnp.dot(x_ref[...], w_ref[...], preferred_element_type=F32)

    @pl.when(k == nk - 1)
    def _():
        acc = acc_ref[...]
        if has_res:
            acc = acc + r_ref[...]
        o_ref[...] = acc.astype(o_ref.dtype)


def _matmul(x, w, out_dtype, residual=None):
    m, k = x.shape
    n = w.shape[1]
    tm = _tile(m, 1024)
    tn = _tile(n, 1024)
    tk = _tile(k, 4096 if residual is None else 2048)
    nk = k // tk
    has_res = residual is not None
    in_specs = [pl.BlockSpec((tm, tk), lambda i, j, kk: (i, kk)),
                pl.BlockSpec((tk, tn), lambda i, j, kk: (kk, j))]
    args = [x, w]
    if has_res:
        in_specs.append(pl.BlockSpec((tm, tn), lambda i, j, kk: (i, j)))
        args.append(residual)
    return pl.pallas_call(
        functools.partial(_matmul_body, nk=nk, has_res=has_res),
        grid=(m // tm, n // tn, nk),
        in_specs=in_specs,
        out_specs=pl.BlockSpec((tm, tn), lambda i, j, kk: (i, j)),
        out_shape=jax.ShapeDtypeStruct((m, n), out_dtype),
        scratch_shapes=[pltpu.VMEM((tm, tn), F32)] if nk > 1 else [],
        compiler_params=_params("parallel", "parallel", "arbitrary"),
        name="matmul",
    )(*args)


CONV_HALO = 16


def _ssd_conv_body(x_ref, w_ref, b_ref, o_ref, *, seq, tr, width):
    pad = width // 2
    w = w_ref[...]
    bias = b_ref[...]
    n_chunks = seq // tr

    def chunk(ci, carry):
        r0 = pl.multiple_of(ci * tr, tr)
        main = x_ref[0, pl.ds(r0, tr), :].astype(F32)
        p0 = pl.multiple_of(jnp.maximum(r0 - CONV_HALO, 0), CONV_HALO)
        n0 = pl.multiple_of(jnp.minimum(r0 + tr, seq - CONV_HALO), CONV_HALO)
        prev = x_ref[0, pl.ds(p0, CONV_HALO), :].astype(F32)
        nxt = x_ref[0, pl.ds(n0, CONV_HALO), :].astype(F32)
        prev = jnp.where(ci > 0, prev, 0.0)
        nxt = jnp.where(ci < n_chunks - 1, nxt, 0.0)
        ext = jnp.concatenate([prev, main, nxt], axis=0)
        acc = jnp.zeros_like(main) + bias
        for t in range(width):
            lo = CONV_HALO + t - pad
            acc = acc + ext[lo:lo + tr, :] * w[t:t + 1, :]
        o_ref[0, pl.ds(r0, tr), :] = _silu(acc).astype(o_ref.dtype)
        return carry

    lax.fori_loop(0, n_chunks, chunk, 0)


def _ssd_conv(xbc, conv_w, conv_b):
    b, s, c = xbc.shape
    width = conv_w.shape[0]
    tc = _tile(c, 256)
    tr = _tile(s, 256)
    return pl.pallas_call(
        functools.partial(_ssd_conv_body, seq=s, tr=tr, width=width),
        grid=(b, c // tc),
        in_specs=[pl.BlockSpec((1, s, tc), lambda i, j: (i, 0, j)),
                  pl.BlockSpec((width, tc), lambda i, j: (0, j)),
                  pl.BlockSpec((1, tc), lambda i, j: (0, j))],
        out_specs=pl.BlockSpec((1, s, tc), lambda i, j: (i, 0, j)),
        out_shape=jax.ShapeDtypeStruct((b, s, c), BF16),
        compiler_params=_params("parallel", "parallel"),
        name="ssd_conv",
    )(xbc, conv_w.astype(F32), conv_b.reshape(1, c).astype(F32))


def _hi_lo(x, axis):
    hi = x.astype(BF16)
    lo = (x - hi.astype(F32)).astype(BF16)
    return jnp.concatenate([hi, lo], axis=axis)


def _ssd_scan_body(x_ref, b_ref, c_ref, dtg_ref, dta_ref, ag_ref, aa_ref, y_ref, state_ref, *,
                   reverse, heads, hdim):
    l = x_ref.shape[1]
    n = SSD_STATE
    width = heads * hdim
    gps = x_ref.shape[2] // width
    all_heads = dta_ref.shape[1]
    nt = (((1,), (1,)), ((), ()))

    @pl.when(pl.program_id(2) == 0)
    def _():
        state_ref[...] = jnp.zeros_like(state_ref)

    row = lax.broadcasted_iota(jnp.int32, (l, l), 0)
    col = lax.broadcasted_iota(jnp.int32, (l, l), 1)
    causal = (col >= row) if reverse else (col <= row)
    tri = causal.astype(BF16)
    tri2 = jnp.concatenate([tri, tri], axis=1)
    eye = (row == col).astype(BF16)

    la_g2 = _hi_lo(dtg_ref[0] * ag_ref[...], 1)
    cum_g = lax.dot_general(tri2, la_g2, nt, preferred_element_type=F32)
    cum_tg = lax.dot_general(la_g2, tri2, nt, preferred_element_type=F32)

    dt_a = dta_ref[0]
    cum_all2 = _hi_lo(lax.dot_general(tri2, _hi_lo(dt_a * aa_ref[...], 1), nt, preferred_element_type=F32), 1)
    dt_all = lax.dot_general(eye, dt_a.astype(BF16), nt, preferred_element_type=F32).astype(BF16)
    hh = lax.broadcasted_iota(jnp.int32, (all_heads, width), 0)
    cc = lax.broadcasted_iota(jnp.int32, (all_heads, width), 1)
    first = lax.broadcasted_iota(jnp.int32, (l, 2 * hdim), 1) < hdim
    zero = jnp.zeros((l, 2 * hdim), BF16)
    last = 0 if reverse else l - 1

    for gi in range(gps):
        group = pl.program_id(1) * gps + gi
        expand = (hh == group * heads + cc // hdim).astype(BF16)
        cum_x = jnp.dot(cum_all2, jnp.concatenate([expand, expand], axis=0),
                        preferred_element_type=F32)
        dt_x = jnp.dot(dt_all, expand, preferred_element_type=F32)
        tot_x = cum_x[last:last + 1, :]
        cum = cum_g[:, gi * heads:(gi + 1) * heads]
        cum_t = cum_tg[gi * heads:(gi + 1) * heads, :]

        x = x_ref[0, :, gi * width:(gi + 1) * width].astype(F32)
        xd = x * dt_x
        bm = b_ref[0, :, gi * n:(gi + 1) * n]
        cm = c_ref[0, :, gi * n:(gi + 1) * n]
        g = lax.dot_general(cm, bm, nt, preferred_element_type=F32)
        xd_b = xd.astype(BF16)

        def weights(h):
            d = cum[:, h:h + 1] - cum_t[h:h + 1, :]
            return (g * jnp.exp(jnp.where(causal, d, -1e30))).astype(BF16)

        ys = []
        for hp in range(heads // 2):
            pair = xd_b[:, 2 * hp * hdim:(2 * hp + 2) * hdim]
            rhs = jnp.concatenate([jnp.where(first, pair, zero), jnp.where(first, zero, pair)], axis=0)
            lhs = jnp.concatenate([weights(2 * hp), weights(2 * hp + 1)], axis=1)
            ys.append(jnp.dot(lhs, rhs, preferred_element_type=F32))
        y = jnp.concatenate(ys, axis=1)
        state = state_ref[gi]
        y = y + jnp.dot(cm, state.astype(BF16), preferred_element_type=F32) * jnp.exp(cum_x)
        y_ref[0, :, gi * width:(gi + 1) * width] = y.astype(y_ref.dtype)
        xdd = (xd * jnp.exp(tot_x - cum_x)).astype(BF16)
        upd = lax.dot_general(bm, xdd, (((0,), (0,)), ((), ())), preferred_element_type=F32)
        state_ref[gi] = state * jnp.exp(tot_x) + upd


SSD_GROUPS_PER_STEP = 4


def _ssd_scan(xbc, inner, dt_t, a, *, reverse):
    b, s, _ = xbc.shape
    groups = SSD_GROUPS
    gps = SSD_GROUPS_PER_STEP
    n = SSD_STATE
    width = inner // groups
    b_off = inner // (gps * n)
    c_off = b_off + groups // gps
    heads = width // SSD_HEAD_DIM
    l = SSD_CHUNK
    nc = s // l

    def cidx(c):
        return nc - 1 - c if reverse else c

    all_heads = dt_t.shape[1]
    a_col = a.reshape(-1, 1).astype(F32)
    return pl.pallas_call(
        functools.partial(_ssd_scan_body, reverse=reverse, heads=heads, hdim=SSD_HEAD_DIM),
        grid=(b, groups // gps, nc),
        in_specs=[pl.BlockSpec((1, l, gps * width), lambda i, g, c: (i, cidx(c), g)),
                  pl.BlockSpec((1, l, gps * n), lambda i, g, c: (i, cidx(c), b_off + g)),
                  pl.BlockSpec((1, l, gps * n), lambda i, g, c: (i, cidx(c), c_off + g)),
                  pl.BlockSpec((1, gps * heads, l), lambda i, g, c: (i, g, cidx(c))),
                  pl.BlockSpec((1, all_heads, l), lambda i, g, c: (i, 0, cidx(c))),
                  pl.BlockSpec((gps * heads, 1), lambda i, g, c: (g, 0)),
                  pl.BlockSpec((all_heads, 1), lambda i, g, c: (0, 0))],
        out_specs=pl.BlockSpec((1, l, gps * width), lambda i, g, c: (i, cidx(c), g)),
        out_shape=jax.ShapeDtypeStruct((b, s, inner), BF16),
        scratch_shapes=[pltpu.VMEM((gps, n, width), F32)],
        compiler_params=_params("parallel", "parallel", "arbitrary"),
        name="ssd_scan_bwd" if reverse else "ssd_scan_fwd",
    )(xbc, xbc, xbc, dt_t, dt_t, a_col, a_col)


def _ssd_gate_body(yf_ref, yb_ref, x_ref, z_ref, d_ref, g_ref, o_ref):
    x = x_ref[...].astype(F32)
    y = yf_ref[...].astype(F32) + yb_ref[...].astype(F32) + d_ref[...] * x
    y = y * _silu(z_ref[...].astype(F32))
    ms = jnp.mean(y * y, axis=-1, keepdims=True)
    o_ref[...] = (y * lax.rsqrt(ms + RMS_EPS) * g_ref[...]).astype(o_ref.dtype)


def _ssd_gate(yf, yb, xbc, z, d_x, norm_g):
    m, inner = yf.shape
    width = inner // SSD_GROUPS
    tm = _tile(m, 512)
    blk = pl.BlockSpec((tm, width), lambda i, g: (i, g))
    vec = pl.BlockSpec((1, width), lambda i, g: (0, g))
    return pl.pallas_call(
        _ssd_gate_body,
        grid=(m // tm, SSD_GROUPS),
        in_specs=[blk, blk, blk, blk, vec, vec],
        out_specs=blk,
        out_shape=jax.ShapeDtypeStruct((m, inner), BF16),
        compiler_params=_params("parallel", "parallel"),
        name="ssd_gate",
    )(yf, yb, xbc, z, d_x.reshape(1, inner).astype(F32), norm_g.reshape(1, inner).astype(F32))


def _ssd_mixer(h, hn, bsz, seq, w_in, conv_w, conv_b, dt_bias, a_log, d_skip, norm_g, w_out):
    m, d = hn.shape
    inner = w_out.shape[0]
    heads = inner // SSD_HEAD_DIM
    gn = SSD_GROUPS * SSD_STATE
    conv_dim = inner + 2 * gn
    w_in = w_in.astype(BF16)
    z = _matmul(hn, w_in[:, :inner], BF16)
    xbc = _matmul(hn, w_in[:, inner:inner + conv_dim], BF16)
    dt_raw = _matmul(hn, w_in[:, inner + conv_dim:], F32)
    xbc = _ssd_conv(xbc.reshape(bsz, seq, conv_dim), conv_w, conv_b)
    dt = jax.nn.softplus(dt_raw.reshape(bsz, seq, 2, heads) + dt_bias.astype(F32))
    dt_t = jnp.transpose(dt, (2, 0, 3, 1))
    a = -jnp.exp(a_log.astype(F32))
    y_f = _ssd_scan(xbc, inner, dt_t[0], a[0], reverse=False)
    y_b = _ssd_scan(xbc, inner, dt_t[1], a[1], reverse=True)
    d_x = jnp.repeat(d_skip.astype(F32), SSD_HEAD_DIM)
    y = _ssd_gate(y_f.reshape(m, inner), y_b.reshape(m, inner), xbc.reshape(m, conv_dim), z, d_x, norm_g)
    return _matmul(y, w_out.astype(BF16), F32, residual=h)


def _hg_scan_body(q_ref, f_ref, v_ref, lb_ref, o_ref, state_ref, *, reverse, chunk):
    r = q_ref.shape[1]
    nch = r // chunk
    dk = HG_KDIM
    nheads = q_ref.shape[2] // dk
    nt = (((1,), (1,)), ((), ()))
    tn = (((0,), (0,)), ((), ()))

    @pl.when(pl.program_id(2) == 0)
    def _():
        state_ref[...] = jnp.zeros_like(state_ref)

    row = lax.broadcasted_iota(jnp.int32, (r, r), 0)
    col = lax.broadcasted_iota(jnp.int32, (r, r), 1)
    same = (row // chunk) == (col // chunk)
    causal = same & ((col >= row) if reverse else (col <= row))
    m_cum = causal.astype(BF16)
    m_all = same.astype(BF16)
    order = range(nch - 1, -1, -1) if reverse else range(nch)

    for hh in range(nheads):
        cols = slice(hh * dk, (hh + 1) * dk)
        lb = lb_ref[hh]
        f = lb + (1.0 - lb) * jax.nn.sigmoid(f_ref[0, :, cols])
        logf = jnp.log(f)
        kk = 1.0 - f
        q = _silu(q_ref[0, :, cols].astype(F32))
        v = v_ref[0, :, cols]
        cum = _split_dot(m_cum, logf)
        tot = _split_dot(m_all, logf)
        q_in = (q * jnp.exp(cum)).astype(BF16)
        k_in = (kk * jnp.exp(-cum)).astype(BF16)
        k_out = (kk * jnp.exp(tot - cum)).astype(BF16)
        e_tot = jnp.exp(tot)
        att = lax.dot_general(q_in, k_in, nt, preferred_element_type=F32)
        att = jnp.where(causal, att, 0.0).astype(BF16)
        o_intra = jnp.dot(att, v, preferred_element_type=F32)

        upd = [lax.dot_general(v[c * chunk:(c + 1) * chunk], k_out[c * chunk:(c + 1) * chunk], tn,
                               preferred_element_type=F32) for c in range(nch)]
        st = state_ref[hh]
        entering = [None] * nch
        for c in order:
            entering[c] = st.astype(BF16)
            st = st * e_tot[c * chunk:c * chunk + 1, :] + upd[c]
        state_ref[hh] = st
        outs = [lax.dot_general(q_in[c * chunk:(c + 1) * chunk], entering[c], nt, preferred_element_type=F32)
                for c in range(nch)]
        o_ref[0, :, cols] = (o_intra + jnp.concatenate(outs, axis=0)).astype(o_ref.dtype)


HG_HEADS_PER_STEP = 2


def _hg_scan(q, f, v, lb, *, reverse):
    b, s, wdt = q.shape
    heads = wdt // HG_KDIM
    hps = HG_HEADS_PER_STEP
    r = _tile(s, 256)
    nb = s // r

    def ridx(c):
        return nb - 1 - c if reverse else c

    blk = pl.BlockSpec((1, r, hps * HG_KDIM), lambda i, h, c: (i, ridx(c), h))
    return pl.pallas_call(
        functools.partial(_hg_scan_body, reverse=reverse, chunk=HG_CHUNK),
        grid=(b, heads // hps, nb),
        in_specs=[blk, blk, blk, pl.BlockSpec((hps, 1, HG_KDIM), lambda i, h, c: (h, 0, 0))],
        out_specs=blk,
        out_shape=jax.ShapeDtypeStruct((b, s, wdt), BF16),
        scratch_shapes=[pltpu.VMEM((hps, HG_KDIM, HG_KDIM), F32)],
        compiler_params=_params("parallel", "parallel", "arbitrary"),
        name="hg_scan_bwd" if reverse else "hg_scan_fwd",
    )(q, f, v, lb)


def _hg_gate_body(of_ref, ob_ref, gate_ref, g_ref, o_ref):
    o = of_ref[...].astype(F32) + ob_ref[...].astype(F32)
    ms = jnp.mean(o * o, axis=-1, keepdims=True)
    o = o * lax.rsqrt(ms + RMS_EPS) * g_ref[...]
    o_ref[...] = (o * _silu(gate_ref[...].astype(F32))).astype(o_ref.dtype)


def _hg_gate(o_f, o_b, gate, norm_g):
    m, wdt = o_f.shape
    heads = wdt // HG_KDIM
    tm = _tile(m, 1024)
    blk = pl.BlockSpec((tm, HG_KDIM), lambda i, h: (i, h))
    return pl.pallas_call(
        _hg_gate_body,
        grid=(m // tm, heads),
        in_specs=[blk, blk, blk, pl.BlockSpec((1, HG_KDIM), lambda i, h: (0, 0))],
        out_specs=blk,
        out_shape=jax.ShapeDtypeStruct((m, wdt), BF16),
        compiler_params=_params("parallel", "parallel"),
        name="hg_gate",
    )(o_f, o_b, gate, norm_g.reshape(1, HG_KDIM).astype(F32))


def _hgrn2_mixer(h, hn, bsz, seq, w_in, lower_bound, norm_g, w_out):
    m, d = hn.shape
    wdt = w_out.shape[0]
    heads = wdt // HG_KDIM
    w_in = w_in.astype(BF16)
    q = _matmul(hn, w_in[:, :wdt], BF16).reshape(bsz, seq, wdt)
    f_fwd = _matmul(hn, w_in[:, wdt:2 * wdt], F32).reshape(bsz, seq, wdt)
    f_bwd = _matmul(hn, w_in[:, 2 * wdt:3 * wdt], F32).reshape(bsz, seq, wdt)
    v = _matmul(hn, w_in[:, 3 * wdt:4 * wdt], BF16).reshape(bsz, seq, wdt)
    gate = _matmul(hn, w_in[:, 4 * wdt:], BF16)
    lb = lower_bound.astype(F32).reshape(heads, 1, HG_KDIM)
    o_f = _hg_scan(q, f_fwd, v, lb, reverse=False)
    o_b = _hg_scan(q, f_bwd, v, lb, reverse=True)
    o = _hg_gate(o_f.reshape(m, wdt), o_b.reshape(m, wdt), gate, norm_g)
    return _matmul(o, w_out.astype(BF16), F32, residual=h)


def _rope_matmul_body(x_ref, w_ref, cos_ref, sin_ref, o_ref, *, n_rope_tiles, hd):
    acc = jnp.dot(x_ref[...], w_ref[...], preferred_element_type=F32)
    j = pl.program_id(1)

    @pl.when(j < n_rope_tiles)
    def _():
        cos = cos_ref[...]
        sin = sin_ref[...]
        for c0 in range(0, acc.shape[1], hd):
            x = acc[:, c0:c0 + hd]
            o_ref[:, c0:c0 + hd] = (x * cos + pltpu.roll(x, hd // 2, axis=1) * sin).astype(o_ref.dtype)

    @pl.when(j >= n_rope_tiles)
    def _():
        o_ref[...] = acc.astype(o_ref.dtype)


def _rope_matmul(x, w, cos_m, sin_m, n_rope_cols):
    m, k = x.shape
    n = w.shape[1]
    tm = _tile(m, 1024)
    tn = _tile(math.gcd(n, n_rope_cols), 1024)
    tab = pl.BlockSpec((tm, AT_HEAD_DIM), lambda i, j: (i, 0))
    return pl.pallas_call(
        functools.partial(_rope_matmul_body, n_rope_tiles=n_rope_cols // tn, hd=AT_HEAD_DIM),
        grid=(m // tm, n // tn),
        in_specs=[pl.BlockSpec((tm, k), lambda i, j: (i, 0)), pl.BlockSpec((k, tn), lambda i, j: (0, j)), tab, tab],
        out_specs=pl.BlockSpec((tm, tn), lambda i, j: (i, j)),
        out_shape=jax.ShapeDtypeStruct((m, n), BF16),
        compiler_params=_params("parallel", "parallel"),
        name="rope_matmul",
    )(x, w, cos_m, sin_m)


def _win_attn_body(q_ref, kp_ref, kc_ref, kn_ref, vp_ref, vc_ref, vn_ref, sink_ref, o_ref, *, rep, hd, nb):
    n = pl.program_id(2)
    blk = q_ref.shape[1]
    kcat = jnp.concatenate([kp_ref[0], kc_ref[0], kn_ref[0]], axis=0)
    vcat = jnp.concatenate([vp_ref[0], vc_ref[0], vn_ref[0]], axis=0)
    rows = rep * blk
    ri = lax.broadcasted_iota(jnp.int32, (rows, 3 * blk), 0)
    qi = ri % blk + blk
    kj = lax.broadcasted_iota(jnp.int32, (rows, 3 * blk), 1)
    valid = jnp.abs(qi - kj) <= blk
    valid = valid & ((kj >= blk) | (n > 0)) & ((kj < 2 * blk) | (n < nb - 1))
    head_of_row = lax.broadcasted_iota(jnp.int32, (rows, 1), 0) // blk
    sink = jnp.zeros((rows, 1), F32)
    for r in range(rep):
        sink = jnp.where(head_of_row == r, sink_ref[0, :, r:r + 1], sink)
    q = jnp.concatenate([q_ref[0, :, r * hd:(r + 1) * hd] for r in range(rep)], axis=0)
    s = lax.dot_general(q, kcat, (((1,), (1,)), ((), ())), preferred_element_type=F32) * (hd ** -0.5)
    s = jnp.where(valid, s, -jnp.inf)
    mx = jnp.maximum(jnp.max(s, axis=-1, keepdims=True), sink)
    p = jnp.exp(s - mx)
    denom = jnp.sum(p, axis=-1, keepdims=True) + jnp.exp(sink - mx)
    o = jnp.dot((p / denom).astype(BF16), vcat, preferred_element_type=F32)
    for r in range(rep):
        o_ref[0, :, r * hd:(r + 1) * hd] = o[r * blk:(r + 1) * blk].astype(o_ref.dtype)


def _win_attn(qkv, qw, sink):
    qk = qkv
    b, s, _ = qk.shape
    hd = AT_HEAD_DIM
    hkv = AT_KV_HEADS
    rep = qw // hd // hkv
    k_off = qw // hd
    v_off = k_off + hkv
    blk = AT_BLOCK
    nb = s // blk
    qspec = pl.BlockSpec((1, blk, rep * hd), lambda i, g, n: (i, n, g))

    def kv(off, col):
        return pl.BlockSpec((1, blk, hd), lambda i, g, n: (i, jnp.clip(n + off, 0, nb - 1), col + g))

    return pl.pallas_call(
        functools.partial(_win_attn_body, rep=rep, hd=hd, nb=nb),
        grid=(b, hkv, nb),
        in_specs=[qspec, kv(-1, k_off), kv(0, k_off), kv(1, k_off), kv(-1, v_off), kv(0, v_off), kv(1, v_off),
                  pl.BlockSpec((1, 1, rep), lambda i, g, n: (g, 0, 0))],
        out_specs=qspec,
        out_shape=jax.ShapeDtypeStruct((b, s, qw), BF16),
        compiler_params=_params("parallel", "parallel", "parallel"),
        name="win_attn",
    )(qk, qk, qk, qk, qkv, qkv, qkv, sink.astype(F32).reshape(hkv, 1, rep))


def _window_gqa_mixer(h, hn, bsz, seq, w_qkv, sink, w_out):
    m, d = hn.shape
    qw = w_out.shape[0]
    kvw = AT_KV_HEADS * AT_HEAD_DIM
    half = AT_HEAD_DIM // 2
    inv = ROPE_THETA ** (-jnp.arange(half, dtype=F32) / half)
    ang = jnp.arange(seq, dtype=F32)[:, None] * inv[None, :]
    cos_m = jnp.tile(jnp.concatenate([jnp.cos(ang), jnp.cos(ang)], axis=1), (bsz, 1))
    sin_m = jnp.tile(jnp.concatenate([-jnp.sin(ang), jnp.sin(ang)], axis=1), (bsz, 1))
    qkv = _rope_matmul(hn, w_qkv.astype(BF16), cos_m, sin_m, qw + kvw).reshape(bsz, seq, qw + 2 * kvw)
    o = _win_attn(qkv, qw, sink)
    return _matmul(o.reshape(m, qw), w_out.astype(BF16), F32, residual=h)


def _cross_attn_body(h_ref, g_ref, wq_ref, kv_ref, wo_ref, o_ref, *, heads, hd):
    hres = h_ref[0]
    ms = jnp.mean(hres * hres, axis=-1, keepdims=True)
    xn = (hres * lax.rsqrt(ms + RMS_EPS) * g_ref[...]).astype(BF16)
    q = jnp.dot(xn, wq_ref[...], preferred_element_type=F32).astype(BF16)
    kv = kv_ref[0]
    scale = hd ** -0.5
    outs = []
    for a in range(heads):
        k = kv[:, a * hd:(a + 1) * hd]
        v = kv[:, (heads + a) * hd:(heads + a + 1) * hd]
        s = lax.dot_general(q[:, a * hd:(a + 1) * hd], k, (((1,), (1,)), ((), ())),
                            preferred_element_type=F32) * scale
        mx = jnp.max(s, axis=-1, keepdims=True)
        p = jnp.exp(s - mx)
        p = p / jnp.sum(p, axis=-1, keepdims=True)
        outs.append(jnp.dot(p.astype(BF16), v, preferred_element_type=F32))
    o = jnp.concatenate(outs, axis=1).astype(BF16)
    o_ref[0] = hres + jnp.dot(o, wo_ref[...], preferred_element_type=F32)


def _cross_attention(h, mem, bsz, seq, norm_x, norm_m, w_q, w_kv, w_out):
    m, d = h.shape
    mt = mem.shape[1]
    cw = w_q.shape[1]
    memn = _rmsnorm(mem.reshape(bsz * mt, d), norm_m, BF16)
    kv = _matmul(memn, w_kv.astype(BF16), BF16).reshape(bsz, mt, 2 * cw)
    tm = _tile(seq, 256)
    out = pl.pallas_call(
        functools.partial(_cross_attn_body, heads=CA_HEADS, hd=CA_HEAD_DIM),
        grid=(bsz, seq // tm),
        in_specs=[pl.BlockSpec((1, tm, d), lambda i, j: (i, j, 0)),
                  pl.BlockSpec((1, d), lambda i, j: (0, 0)),
                  pl.BlockSpec((d, cw), lambda i, j: (0, 0)),
                  pl.BlockSpec((1, mt, 2 * cw), lambda i, j: (i, 0, 0)),
                  pl.BlockSpec((cw, d), lambda i, j: (0, 0))],
        out_specs=pl.BlockSpec((1, tm, d), lambda i, j: (i, j, 0)),
        out_shape=jax.ShapeDtypeStruct((bsz, seq, d), F32),
        compiler_params=_params("parallel", "parallel"),
        name="cross_attn",
    )(h.reshape(bsz, seq, d), norm_x.reshape(1, d).astype(F32), w_q.astype(BF16), kv, w_out.astype(BF16))
    return out.reshape(m, d)


MOE_ROW_BLOCK = 512
MOE_HIDDEN_TILE = 512
MOE_TOKEN_TILE = 256


HIGH_HALF = -65536
DMA_LOOP_UNROLL = 8


def _pack_pairs(x):
    half = x.shape[1] // 2
    bits = lax.bitcast_convert_type(x.astype(BF16).astype(F32), jnp.int32)
    return lax.shift_right_logical(bits[:, :half], jnp.int32(16)) | bits[:, half:]


def _unpack_pairs(p):
    lo = lax.bitcast_convert_type(lax.shift_left(p, jnp.int32(16)), F32)
    hi = lax.bitcast_convert_type(p & jnp.int32(HIGH_HALF), F32)
    return lo, hi


def _moe_norm_logits_body(h_ref, g_ref, wr_ref, xp_ref, lg_ref):
    x = h_ref[...]
    ms = jnp.mean(x * x, axis=-1, keepdims=True)
    xn = x * lax.rsqrt(ms + RMS_EPS) * g_ref[...]
    xp_ref[...] = _pack_pairs(xn)
    lg_ref[...] = jnp.dot(xn, wr_ref[...], preferred_element_type=F32, precision=lax.Precision.HIGHEST)


def _moe_norm_logits(h, g, w_route):
    m, d = h.shape
    tm = _tile(m, 256)
    return pl.pallas_call(
        _moe_norm_logits_body,
        grid=(m // tm,),
        in_specs=[pl.BlockSpec((tm, d), lambda i: (i, 0)),
                  pl.BlockSpec((1, d), lambda i: (0, 0)),
                  pl.BlockSpec((d, LANES), lambda i: (0, 0))],
        out_specs=[pl.BlockSpec((tm, d // 2), lambda i: (i, 0)), pl.BlockSpec((tm, LANES), lambda i: (i, 0))],
        out_shape=[jax.ShapeDtypeStruct((m, d // 2), jnp.int32), jax.ShapeDtypeStruct((m, LANES), F32)],
        compiler_params=_params("parallel"),
        name="moe_norm_logits",
    )(h, g.reshape(1, d).astype(F32), w_route)


def _row_copy(src_hbm, dst_ref, sem, src_row, dst_row):
    return pltpu.make_async_copy(src_hbm.at[pl.ds(src_row, 1)], dst_ref.at[pl.ds(dst_row, 1)], sem)


def _gather_rows_body(idx_ref, src_hbm, o_ref, sem, *, tm):
    base = pl.program_id(0) * tm

    def start(r, carry):
        _row_copy(src_hbm, o_ref, sem, idx_ref[base + r], r).start()
        return carry

    def wait(r, carry):
        _row_copy(src_hbm, o_ref, sem, 0, r).wait()
        return carry

    lax.fori_loop(0, tm, start, 0, unroll=DMA_LOOP_UNROLL)
    lax.fori_loop(0, tm, wait, 0, unroll=DMA_LOOP_UNROLL)


def _gather_rows(src, idx):
    rows = idx.shape[0]
    d = src.shape[1]
    tm = _tile(rows, 512)
    return pl.pallas_call(
        functools.partial(_gather_rows_body, tm=tm),
        grid_spec=pltpu.PrefetchScalarGridSpec(
            num_scalar_prefetch=1,
            grid=(rows // tm,),
            in_specs=[pl.BlockSpec(memory_space=pl.ANY)],
            out_specs=pl.BlockSpec((tm, d), lambda i, idx_ref: (i, 0)),
            scratch_shapes=[pltpu.SemaphoreType.DMA(())]),
        out_shape=jax.ShapeDtypeStruct((rows, d), src.dtype),
        compiler_params=_params("arbitrary"),
        name="moe_gather",
    )(idx, src)


def _experts_body(be_ref, nused_ref, x_ref, wg_ref, wu_ref, wd_ref, o_ref, *acc, nj):
    i = pl.program_id(0)
    j = pl.program_id(1)
    used = i < nused_ref[0]

    @pl.when(jnp.logical_and(j == nj - 1, jnp.logical_not(used)))
    def _():
        o_ref[...] = jnp.zeros_like(o_ref)

    @pl.when(used)
    def _():
        lo, hi = _unpack_pairs(x_ref[...])
        x = jnp.concatenate([lo.astype(BF16), hi.astype(BF16)], axis=1)
        hid = (_silu(jnp.dot(x, wg_ref[0], preferred_element_type=F32))
               * jnp.dot(x, wu_ref[0], preferred_element_type=F32)).astype(BF16)
        part = jnp.dot(hid, wd_ref[0], preferred_element_type=F32)
        if nj == 1:
            o_ref[...] = _pack_pairs(part)
            return
        acc_ref, = acc

        @pl.when(j == 0)
        def _():
            acc_ref[...] = part

        @pl.when(jnp.logical_and(j > 0, j < nj - 1))
        def _():
            acc_ref[...] += part

        @pl.when(j == nj - 1)
        def _():
            o_ref[...] = _pack_pairs(acc_ref[...] + part)


def _experts(buf, blk_expert, n_used, w_gate, w_up, w_down):
    rows, dh = buf.shape
    d = 2 * dh
    hidden = w_gate.shape[2]
    tm = MOE_ROW_BLOCK
    th = _tile(hidden, MOE_HIDDEN_TILE)
    nj = hidden // th
    nblk = rows // tm

    def xmap(i, j, be, nu):
        return (jnp.minimum(i, nu[0] - 1), 0)

    return pl.pallas_call(
        functools.partial(_experts_body, nj=nj),
        grid_spec=pltpu.PrefetchScalarGridSpec(
            num_scalar_prefetch=2,
            grid=(nblk, nj),
            in_specs=[pl.BlockSpec((tm, dh), xmap),
                      pl.BlockSpec((1, d, th), lambda i, j, be, nu: (be[i], 0, j)),
                      pl.BlockSpec((1, d, th), lambda i, j, be, nu: (be[i], 0, j)),
                      pl.BlockSpec((1, th, d), lambda i, j, be, nu: (be[i], j, 0))],
            out_specs=pl.BlockSpec((tm, dh), lambda i, j, be, nu: (i, 0)),
            scratch_shapes=[pltpu.VMEM((tm, d), F32)] if nj > 1 else []),
        out_shape=jax.ShapeDtypeStruct((rows, dh), jnp.int32),
        compiler_params=_params("arbitrary", "arbitrary"),
        name="moe_experts",
    )(blk_expert, n_used, buf, w_gate, w_up, w_down)


def _combine_body(d0_ref, d1_ref, src_hbm, h_ref, gate_ref, o_ref, rows0, rows1, sem, *, tm):
    base = pl.program_id(0) * tm

    def start(r, carry):
        _row_copy(src_hbm, rows0, sem.at[0], d0_ref[base + r], r).start()
        _row_copy(src_hbm, rows1, sem.at[1], d1_ref[base + r], r).start()
        return carry

    def wait(r, carry):
        _row_copy(src_hbm, rows0, sem.at[0], 0, r).wait()
        _row_copy(src_hbm, rows1, sem.at[1], 0, r).wait()
        return carry

    lax.fori_loop(0, tm, start, 0, unroll=DMA_LOOP_UNROLL)
    lax.fori_loop(0, tm, wait, 0, unroll=DMA_LOOP_UNROLL)
    g = gate_ref[...]
    half = rows0.shape[1]
    a_lo, a_hi = _unpack_pairs(rows0[...])
    b_lo, b_hi = _unpack_pairs(rows1[...])
    o_ref[:, :half] = h_ref[:, :half] + a_lo * g[:, 0:1] + b_lo * g[:, 1:2]
    o_ref[:, half:] = h_ref[:, half:] + a_hi * g[:, 0:1] + b_hi * g[:, 1:2]


def _combine(h, out_buf, dest, gates):
    m, d = h.shape
    tm = _tile(m, MOE_TOKEN_TILE)
    return pl.pallas_call(
        functools.partial(_combine_body, tm=tm),
        grid_spec=pltpu.PrefetchScalarGridSpec(
            num_scalar_prefetch=2,
            grid=(m // tm,),
            in_specs=[pl.BlockSpec(memory_space=pl.ANY),
                      pl.BlockSpec((tm, d), lambda i, a, b: (i, 0)),
                      pl.BlockSpec((tm, MOE_TOP_K), lambda i, a, b: (i, 0))],
            out_specs=pl.BlockSpec((tm, d), lambda i, a, b: (i, 0)),
            scratch_shapes=[pltpu.VMEM((tm, d // 2), jnp.int32), pltpu.VMEM((tm, d // 2), jnp.int32),
                            pltpu.SemaphoreType.DMA((2,))]),
        out_shape=jax.ShapeDtypeStruct((m, d), F32),
        compiler_params=_params("arbitrary"),
        name="moe_combine",
    )(dest[:, 0], dest[:, 1], out_buf, h, gates)


def _hier_moe(h, norm_g, w_group, b_group, w_expert, b_expert, w_gate, w_up, w_down):
    n, d = h.shape
    w_route = jnp.concatenate([w_group, w_expert], axis=1).astype(F32)
    w_route = jnp.pad(w_route, ((0, 0), (0, LANES - w_route.shape[1])))
    xp, logits = _moe_norm_logits(h, norm_g, w_route)
    g_logits = logits[:, :MOE_GROUPS] + b_group.astype(F32)
    g_prob = jax.nn.softmax(g_logits, axis=-1)
    g_sel = jnp.argmax(g_logits, axis=-1)
    g_weight = jnp.take_along_axis(g_prob, g_sel[:, None], axis=-1)
    e_logits = (logits[:, MOE_GROUPS:MOE_GROUPS + MOE_EXPERTS] + b_expert.astype(F32)
                ).reshape(n, MOE_GROUPS, MOE_EXPERTS_PER_GROUP)
    e_logits = jnp.take_along_axis(e_logits, g_sel[:, None, None], axis=1)[:, 0]
    top_val, top_idx = lax.top_k(e_logits, MOE_TOP_K)
    gates = g_weight * jax.nn.softmax(top_val, axis=-1)
    experts = (g_sel[:, None] * MOE_EXPERTS_PER_GROUP + top_idx).astype(jnp.int32)

    nk = n * MOE_TOP_K
    tm = MOE_ROW_BLOCK
    e_flat = experts.reshape(nk)
    onehot = (e_flat[:, None] == jnp.arange(MOE_EXPERTS, dtype=jnp.int32)[None, :]).astype(jnp.int32)
    rank = jnp.sum((jnp.cumsum(onehot, axis=0) - onehot) * onehot, axis=1)
    counts = jnp.sum(onehot, axis=0)
    padded = (counts + tm - 1) // tm * tm
    pend = jnp.cumsum(padded)
    pstart = pend - padded
    dest = (pstart[e_flat] + rank).astype(jnp.int32)
    n_blocks = -(-nk // tm) + MOE_EXPERTS
    rows = n_blocks * tm
    src = jnp.zeros((rows,), jnp.int32).at[dest].set(jnp.arange(nk, dtype=jnp.int32) // MOE_TOP_K)
    blk_expert = jnp.minimum(jnp.searchsorted(pend, jnp.arange(n_blocks, dtype=jnp.int32) * tm, side='right'),
                             MOE_EXPERTS - 1).astype(jnp.int32)
    n_used = (pend[-1] // tm).astype(jnp.int32).reshape(1)

    buf = _gather_rows(xp, src)
    out_buf = _experts(buf, blk_expert, n_used, w_gate.astype(BF16), w_up.astype(BF16), w_down.astype(BF16))
    return _combine(h, out_buf, dest.reshape(n, MOE_TOP_K), gates.astype(F32))


def _trunk(x, mem, p):
    bsz, seq, d = x.shape
    m = bsz * seq
    depth = p['norm_mix'].shape[0]
    lb_soft = jax.nn.softmax(p['hg_lb'].astype(F32), axis=0)
    lower_bounds = jnp.cumsum(lb_soft, axis=0) - lb_soft[0]
    h = x.reshape(m, d)
    for i in range(depth):
        kind, j = i % N_MIXERS, i // N_MIXERS
        hn = _rmsnorm(h, p['norm_mix'][i], BF16)
        if kind == 0:
            h = _ssd_mixer(h, hn, bsz, seq, p['ssd_w_in'][j], p['ssd_conv_w'][j], p['ssd_conv_b'][j],
                           p['ssd_dt_bias'][j], p['ssd_a_log'][j], p['ssd_d'][j], p['ssd_norm'][j],
                           p['ssd_w_out'][j])
        elif kind == 1:
            h = _hgrn2_mixer(h, hn, bsz, seq, p['hg_w_in'][j], lower_bounds[i], p['hg_norm'][j], p['hg_w_out'][j])
        else:
            h = _window_gqa_mixer(h, hn, bsz, seq, p['at_w_qkv'][j], p['at_sink'][j], p['at_w_out'][j])
        h = _cross_attention(h, mem, bsz, seq, p['norm_cross'][i], p['norm_mem'][i],
                             p['ca_w_q'][i], p['ca_w_kv'][i], p['ca_w_out'][i])
        h = _hier_moe(h, p['norm_moe'][i], p['moe_w_group'][i], p['moe_b_group'][i], p['moe_w_expert'][i],
                      p['moe_b_expert'][i], p['moe_w_gate'][i], p['moe_w_up'][i], p['moe_w_down'][i])
    return _rmsnorm(h, p['norm_final'], F32).reshape(bsz, seq, d)


def kernel(x_prompt, x_sample, mem_prompt, mem_sample, norm_mix, norm_cross, norm_mem, norm_moe, norm_final,
           ssd_w_in, ssd_conv_w, ssd_conv_b, ssd_dt_bias, ssd_a_log, ssd_d, ssd_norm, ssd_w_out,
           hg_w_in, hg_lb, hg_norm, hg_w_out, at_w_qkv, at_sink, at_w_out, ca_w_q, ca_w_kv, ca_w_out,
           moe_w_group, moe_b_group, moe_w_expert, moe_b_expert, moe_w_gate, moe_w_up, moe_w_down):
    p = dict(norm_mix=norm_mix, norm_cross=norm_cross, norm_mem=norm_mem, norm_moe=norm_moe,
             norm_final=norm_final, ssd_w_in=ssd_w_in, ssd_conv_w=ssd_conv_w, ssd_conv_b=ssd_conv_b,
             ssd_dt_bias=ssd_dt_bias, ssd_a_log=ssd_a_log, ssd_d=ssd_d, ssd_norm=ssd_norm, ssd_w_out=ssd_w_out,
             hg_w_in=hg_w_in, hg_lb=hg_lb, hg_norm=hg_norm, hg_w_out=hg_w_out,
             at_w_qkv=at_w_qkv, at_sink=at_sink, at_w_out=at_w_out,
             ca_w_q=ca_w_q, ca_w_kv=ca_w_kv, ca_w_out=ca_w_out,
             moe_w_group=moe_w_group, moe_b_group=moe_b_group, moe_w_expert=moe_w_expert,
             moe_b_expert=moe_b_expert, moe_w_gate=moe_w_gate, moe_w_up=moe_w_up, moe_w_down=moe_w_down)
    for name in ('ssd_w_in', 'ssd_w_out', 'hg_w_in', 'hg_w_out', 'at_w_qkv', 'at_w_out', 'ca_w_q', 'ca_w_kv',
                 'ca_w_out', 'moe_w_gate', 'moe_w_up', 'moe_w_down'):
        p[name] = p[name].astype(BF16)
    return (_trunk(x_prompt, mem_prompt, p), _trunk(x_sample, mem_sample, p))
```

```python
import functools
import math

import jax
import jax.numpy as jnp
from jax import lax
from jax.experimental import pallas as pl
from jax.experimental.pallas import tpu as pltpu

F32 = jnp.float32
BF16 = jnp.bfloat16

RMS_EPS = 1e-6
ROPE_THETA = 10000.0
SSD_HEAD_DIM = 64
SSD_GROUPS = 8
SSD_STATE = 128
SSD_CHUNK = 128
HG_KDIM = 128
HG_CHUNK = 16
AT_HEAD_DIM = 128
AT_KV_HEADS = 8
AT_BLOCK = 128
CA_HEADS = 4
CA_HEAD_DIM = 128
MOE_GROUPS = 4
MOE_EXPERTS_PER_GROUP = 4
MOE_EXPERTS = MOE_GROUPS * MOE_EXPERTS_PER_GROUP
MOE_TOP_K = 2
N_MIXERS = 3

V7X_VMEM_LIMIT_BYTES = 56 * 1024 * 1024
LANES = 128


def _params(*sem):
    return pltpu.CompilerParams(dimension_semantics=sem, vmem_limit_bytes=V7X_VMEM_LIMIT_BYTES)


def _tile(n, pref):
    t = min(n, pref)
    while n % t:
        t //= 2
    return t


def _split_dot(m01, x):
    hi = x.astype(BF16)
    lo = (x - hi.astype(F32)).astype(BF16)
    return (jnp.dot(m01, hi, preferred_element_type=F32) + jnp.dot(m01, lo, preferred_element_type=F32))


def _silu(x):
    return x * jax.nn.sigmoid(x)


def _rmsnorm_body(x_ref, g_ref, o_ref):
    x = x_ref[...].astype(F32)
    ms = jnp.mean(x * x, axis=-1, keepdims=True)
    o_ref[...] = (x * lax.rsqrt(ms + RMS_EPS) * g_ref[...]).astype(o_ref.dtype)


def _rmsnorm(x, g, out_dtype):
    m, d = x.shape
    tm = _tile(m, 256)
    return pl.pallas_call(
        _rmsnorm_body,
        grid=(m // tm,),
        in_specs=[pl.BlockSpec((tm, d), lambda i: (i, 0)), pl.BlockSpec((1, d), lambda i: (0, 0))],
        out_specs=pl.BlockSpec((tm, d), lambda i: (i, 0)),
        out_shape=jax.ShapeDtypeStruct((m, d), out_dtype),
        compiler_params=_params("parallel"),
        name="rmsnorm",
    )(x, g.reshape(1, d).astype(F32))


def _matmul_body(*refs, nk, has_res):
    if has_res:
        x_ref, w_ref, r_ref, o_ref = refs[:4]
        scratch = refs[4:]
    else:
        x_ref, w_ref, o_ref = refs[:3]
        r_ref = None
        scratch = refs[3:]
    if nk == 1:
        acc = jnp.dot(x_ref[...], w_ref[...], preferred_element_type=F32)
        if has_res:
            acc = acc + r_ref[...]
        o_ref[...] = acc.astype(o_ref.dtype)
        return
    acc_ref, = scratch
    k = pl.program_id(2)

    @pl.when(k == 0)
    def _():
        acc_ref[...] = jnp.zeros_like(acc_ref)

    acc_ref[...] += jnp.dot(x_ref[...], w_ref[...], preferred_element_type=F32)

    @pl.when(k == nk - 1)
    def _():
        acc = acc_ref[...]
        if has_res:
            acc = acc + r_ref[...]
        o_ref[...] = acc.astype(o_ref.dtype)


def _matmul(x, w, out_dtype, residual=None, col0=0, ncols=None):
    m, k = x.shape
    n = w.shape[1] - col0 if ncols is None else ncols
    tm = _tile(m, 1024)
    tn = _tile(math.gcd(n, col0) if col0 else n, 1024)
    tk = _tile(k, 4096 if residual is None else 2048)
    nk = k // tk
    j0 = col0 // tn
    has_res = residual is not None
    in_specs = [pl.BlockSpec((tm, tk), lambda i, j, kk: (i, kk)),
                pl.BlockSpec((tk, tn), lambda i, j, kk: (kk, j0 + j))]
    args = [x, w]
    if has_res:
        in_specs.append(pl.BlockSpec((tm, tn), lambda i, j, kk: (i, j)))
        args.append(residual)
    return pl.pallas_call(
        functools.partial(_matmul_body, nk=nk, has_res=has_res),
        grid=(m // tm, n // tn, nk),
        in_specs=in_specs,
        out_specs=pl.BlockSpec((tm, tn), lambda i, j, kk: (i, j)),
        out_shape=jax.ShapeDtypeStruct((m, n), out_dtype),
        scratch_shapes=[pltpu.VMEM((tm, tn), F32)] if nk > 1 else [],
        compiler_params=_params("parallel", "parallel", "arbitrary"),
        name="matmul",
    )(*args)


CONV_HALO = 16


def _ssd_conv_body(x_ref, prev_ref, next_ref, w_ref, b_ref, o_ref, ext_ref, *, tr, width):
    pad = width // 2
    ts = x_ref.shape[1]
    j = pl.program_id(1)
    w = w_ref[...]
    bias = b_ref[...]
    ext_ref[pl.ds(0, CONV_HALO), :] = jnp.where(j > 0, prev_ref[0], jnp.zeros_like(prev_ref[0]))
    ext_ref[pl.ds(CONV_HALO, ts), :] = x_ref[0]
    ext_ref[pl.ds(CONV_HALO + ts, CONV_HALO), :] = jnp.where(j < pl.num_programs(1) - 1, next_ref[0],
                                                              jnp.zeros_like(next_ref[0]))

    def chunk(ci, carry):
        r0 = pl.multiple_of(ci * tr, tr)
        ext = ext_ref[pl.ds(r0, tr + 2 * CONV_HALO), :].astype(F32)
        acc = jnp.zeros((tr, ext.shape[1]), F32) + bias
        for t in range(width):
            lo = CONV_HALO + t - pad
            acc = acc + ext[lo:lo + tr, :] * w[t:t + 1, :]
        o_ref[0, pl.ds(r0, tr), :] = _silu(acc).astype(o_ref.dtype)
        return carry

    lax.fori_loop(0, ts // tr, chunk, 0)


def _ssd_conv(xbc, conv_w, conv_b):
    b, s, c = xbc.shape
    width = conv_w.shape[0]
    tc = _tile(c, 1024)
    ts = _tile(s, 512)
    tr = _tile(ts, 32)
    hb = ts // CONV_HALO
    last = s // CONV_HALO - 1
    return pl.pallas_call(
        functools.partial(_ssd_conv_body, tr=tr, width=width),
        grid=(b, s // ts, c // tc),
        in_specs=[pl.BlockSpec((1, ts, tc), lambda i, j, k: (i, j, k)),
                  pl.BlockSpec((1, CONV_HALO, tc), lambda i, j, k: (i, jnp.maximum(j * hb - 1, 0), k)),
                  pl.BlockSpec((1, CONV_HALO, tc), lambda i, j, k: (i, jnp.minimum((j + 1) * hb, last), k)),
                  pl.BlockSpec((width, tc), lambda i, j, k: (0, k)),
                  pl.BlockSpec((1, tc), lambda i, j, k: (0, k))],
        out_specs=pl.BlockSpec((1, ts, tc), lambda i, j, k: (i, j, k)),
        out_shape=jax.ShapeDtypeStruct((b, s, c), BF16),
        scratch_shapes=[pltpu.VMEM((ts + 2 * CONV_HALO, tc), BF16)],
        compiler_params=_params("parallel", "parallel", "parallel"),
        name="ssd_conv",
    )(xbc, xbc, xbc, conv_w.astype(F32), conv_b.reshape(1, c).astype(F32))


def _hi_lo(x, axis):
    hi = x.astype(BF16)
    lo = (x - hi.astype(F32)).astype(BF16)
    return jnp.concatenate([hi, lo], axis=axis)


def _ssd_scan_body(x_ref, b_ref, c_ref, dtg_ref, dta_ref, ag_ref, aa_ref, y_ref, state_ref, *,
                   reverse, heads, hdim):
    l = x_ref.shape[1]
    n = SSD_STATE
    width = heads * hdim
    gps = x_ref.shape[2] // width
    all_heads = dta_ref.shape[1]
    nt = (((1,), (1,)), ((), ()))

    @pl.when(pl.program_id(2) == 0)
    def _():
        state_ref[...] = jnp.zeros_like(state_ref)

    row = lax.broadcasted_iota(jnp.int32, (l, l), 0)
    col = lax.broadcasted_iota(jnp.int32, (l, l), 1)
    causal = (col >= row) if reverse else (col <= row)
    tri = causal.astype(BF16)
    tri2 = jnp.concatenate([tri, tri], axis=1)
    eye = (row == col).astype(BF16)

    la_g2 = _hi_lo(dtg_ref[0] * ag_ref[...], 1)
    cum_g = lax.dot_general(tri2, la_g2, nt, preferred_element_type=F32)
    cum_tg = lax.dot_general(la_g2, tri2, nt, preferred_element_type=F32)

    dt_a = dta_ref[0]
    cum_all2 = _hi_lo(lax.dot_general(tri2, _hi_lo(dt_a * aa_ref[...], 1), nt, preferred_element_type=F32), 1)
    dt_all = lax.dot_general(eye, dt_a.astype(BF16), nt, preferred_element_type=F32).astype(BF16)
    hh = lax.broadcasted_iota(jnp.int32, (all_heads, width), 0)
    cc = lax.broadcasted_iota(jnp.int32, (all_heads, width), 1)
    first = lax.broadcasted_iota(jnp.int32, (l, 2 * hdim), 1) < hdim
    zero = jnp.zeros((l, 2 * hdim), BF16)
    last = 0 if reverse else l - 1

    for gi in range(gps):
        group = pl.program_id(1) * gps + gi
        expand = (hh == group * heads + cc // hdim).astype(BF16)
        cum_x = jnp.dot(cum_all2, jnp.concatenate([expand, expand], axis=0),
                        preferred_element_type=F32)
        dt_x = jnp.dot(dt_all, expand, preferred_element_type=F32)
        tot_x = cum_x[last:last + 1, :]
        cum = cum_g[:, gi * heads:(gi + 1) * heads]
        cum_t = cum_tg[gi * heads:(gi + 1) * heads, :]

        x = x_ref[0, :, gi * width:(gi + 1) * width].astype(F32)
        xd = x * dt_x
        bm = b_ref[0, :, gi * n:(gi + 1) * n]
        cm = c_ref[0, :, gi * n:(gi + 1) * n]
        g = lax.dot_general(cm, bm, nt, preferred_element_type=F32)
        xd_b = xd.astype(BF16)

        def weights(h):
            d = cum[:, h:h + 1] - cum_t[h:h + 1, :]
            return (g * jnp.exp(jnp.where(causal, d, -1e30))).astype(BF16)

        ys = []
        for hp in range(heads // 2):
            pair = xd_b[:, 2 * hp * hdim:(2 * hp + 2) * hdim]
            rhs = jnp.concatenate([jnp.where(first, pair, zero), jnp.where(first, zero, pair)], axis=0)
            lhs = jnp.concatenate([weights(2 * hp), weights(2 * hp + 1)], axis=1)
            ys.append(jnp.dot(lhs, rhs, preferred_element_type=F32))
        y = jnp.concatenate(ys, axis=1)
        state = state_ref[gi]
        y = y + jnp.dot(cm, state.astype(BF16), preferred_element_type=F32) * jnp.exp(cum_x)
        y_ref[0, :, gi * width:(gi + 1) * width] = y.astype(y_ref.dtype)
        xdd = (xd * jnp.exp(tot_x - cum_x)).astype(BF16)
        upd = lax.dot_general(bm, xdd, (((0,), (0,)), ((), ())), preferred_element_type=F32)
        state_ref[gi] = state * jnp.exp(tot_x) + upd


SSD_GROUPS_PER_STEP = 4


def _ssd_scan(xbc, inner, dt_t, a, *, reverse):
    b, s, _ = xbc.shape
    groups = SSD_GROUPS
    gps = SSD_GROUPS_PER_STEP
    n = SSD_STATE
    width = inner // groups
    b_off = inner // (gps * n)
    c_off = b_off + groups // gps
    heads = width // SSD_HEAD_DIM
    l = SSD_CHUNK
    nc = s // l

    def cidx(c):
        return nc - 1 - c if reverse else c

    all_heads = dt_t.shape[1]
    a_col = a.reshape(-1, 1).astype(F32)
    return pl.pallas_call(
        functools.partial(_ssd_scan_body, reverse=reverse, heads=heads, hdim=SSD_HEAD_DIM),
        grid=(b, groups // gps, nc),
        in_specs=[pl.BlockSpec((1, l, gps * width), lambda i, g, c: (i, cidx(c), g)),
                  pl.BlockSpec((1, l, gps * n), lambda i, g, c: (i, cidx(c), b_off + g)),
                  pl.BlockSpec((1, l, gps * n), lambda i, g, c: (i, cidx(c), c_off + g)),
                  pl.BlockSpec((1, gps * heads, l), lambda i, g, c: (i, g, cidx(c))),
                  pl.BlockSpec((1, all_heads, l), lambda i, g, c: (i, 0, cidx(c))),
                  pl.BlockSpec((gps * heads, 1), lambda i, g, c: (g, 0)),
                  pl.BlockSpec((all_heads, 1), lambda i, g, c: (0, 0))],
        out_specs=pl.BlockSpec((1, l, gps * width), lambda i, g, c: (i, cidx(c), g)),
        out_shape=jax.ShapeDtypeStruct((b, s, inner), BF16),
        scratch_shapes=[pltpu.VMEM((gps, n, width), F32)],
        compiler_params=_params("parallel", "parallel", "arbitrary"),
        name="ssd_scan_bwd" if reverse else "ssd_scan_fwd",
    )(xbc, xbc, xbc, dt_t, dt_t, a_col, a_col)


def _ssd_gate_body(yf_ref, yb_ref, x_ref, z_ref, d_ref, g_ref, o_ref):
    x = x_ref[...].astype(F32)
    y = yf_ref[...].astype(F32) + yb_ref[...].astype(F32) + d_ref[...] * x
    y = y * _silu(z_ref[...].astype(F32))
    ms = jnp.mean(y * y, axis=-1, keepdims=True)
    o_ref[...] = (y * lax.rsqrt(ms + RMS_EPS) * g_ref[...]).astype(o_ref.dtype)


def _ssd_gate(yf, yb, xbc, z, d_x, norm_g):
    m, inner = yf.shape
    width = inner // SSD_GROUPS
    tm = _tile(m, 512)
    blk = pl.BlockSpec((tm, width), lambda i, g: (i, g))
    vec = pl.BlockSpec((1, width), lambda i, g: (0, g))
    return pl.pallas_call(
        _ssd_gate_body,
        grid=(m // tm, SSD_GROUPS),
        in_specs=[blk, blk, blk, blk, vec, vec],
        out_specs=blk,
        out_shape=jax.ShapeDtypeStruct((m, inner), BF16),
        compiler_params=_params("parallel", "parallel"),
        name="ssd_gate",
    )(yf, yb, xbc, z, d_x.reshape(1, inner).astype(F32), norm_g.reshape(1, inner).astype(F32))


def _ssd_mixer(h, hn, bsz, seq, w_in, conv_w, conv_b, dt_bias, a_log, d_skip, norm_g, w_out):
    m, d = hn.shape
    inner = w_out.shape[0]
    heads = inner // SSD_HEAD_DIM
    gn = SSD_GROUPS * SSD_STATE
    conv_dim = inner + 2 * gn
    w_in = w_in.astype(BF16)
    z = _matmul(hn, w_in, BF16, col0=0, ncols=inner)
    xbc = _matmul(hn, w_in, BF16, col0=inner, ncols=conv_dim)
    dt_raw = _matmul(hn, w_in, F32, col0=inner + conv_dim)
    xbc = _ssd_conv(xbc.reshape(bsz, seq, conv_dim), conv_w, conv_b)
    dt = jax.nn.softplus(dt_raw.reshape(bsz, seq, 2, heads) + dt_bias.astype(F32))
    dt_t = jnp.transpose(dt, (2, 0, 3, 1))
    a = -jnp.exp(a_log.astype(F32))
    y_f = _ssd_scan(xbc, inner, dt_t[0], a[0], reverse=False)
    y_b = _ssd_scan(xbc, inner, dt_t[1], a[1], reverse=True)
    d_x = jnp.repeat(d_skip.astype(F32), SSD_HEAD_DIM)
    y = _ssd_gate(y_f.reshape(m, inner), y_b.reshape(m, inner), xbc.reshape(m, conv_dim), z, d_x, norm_g)
    return _matmul(y, w_out.astype(BF16), F32, residual=h)


def _hg_scan_body(q_ref, f_ref, v_ref, lb_ref, o_ref, state_ref, *, reverse, chunk):
    r = q_ref.shape[1]
    nch = r // chunk
    dk = HG_KDIM
    nheads = q_ref.shape[2] // dk
    nt = (((1,), (1,)), ((), ()))
    tn = (((0,), (0,)), ((), ()))

    @pl.when(pl.program_id(2) == 0)
    def _():
        state_ref[...] = jnp.zeros_like(state_ref)

    pair = 2 * chunk
    npair = r // pair
    row = lax.broadcasted_iota(jnp.int32, (r, r), 0)
    col = lax.broadcasted_iota(jnp.int32, (r, r), 1)
    same = (row // chunk) == (col // chunk)
    causal = same & ((col >= row) if reverse else (col <= row))
    m_cum = causal.astype(BF16)
    same_pair = (row // pair) == (col // pair)
    band = same_pair & ((col // chunk > row // chunk) if reverse else (col // chunk < row // chunk))
    rid = lax.broadcasted_iota(jnp.int32, (r, 1), 0)
    even_chunk = (rid // chunk) % 2 == 0
    is_b = even_chunk if reverse else jnp.logical_not(even_chunk)
    last = 0 if reverse else chunk - 1
    order = range(npair - 1, -1, -1) if reverse else range(npair)

    def per_row_chunk_total(cum):
        c3 = cum.reshape(nch, chunk, dk)
        return jnp.broadcast_to(c3[:, last:last + 1, :], (nch, chunk, dk)).reshape(r, dk)

    heads = range(nheads)
    vs, q_far, k_far, e_pair, o_intra = [], [], [], [], []
    for hh in heads:
        cols = slice(hh * dk, (hh + 1) * dk)
        lb = lb_ref[hh]
        f = lb + (1.0 - lb) * jax.nn.sigmoid(f_ref[0, :, cols])
        logf = jnp.log(f)
        kk = 1.0 - f
        q = _silu(q_ref[0, :, cols].astype(F32))
        v = v_ref[0, :, cols]
        cum = _split_dot(m_cum, logf)
        tot = per_row_chunk_total(cum)
        partner = jnp.where(even_chunk, jnp.concatenate([tot[chunk:], tot[:chunk]], axis=0),
                            jnp.concatenate([tot[-chunk:], tot[:-chunk]], axis=0))
        q_in = (q * jnp.exp(cum)).astype(BF16)
        k_in = (kk * jnp.exp(-cum)).astype(BF16)
        k_out = (kk * jnp.exp(tot - cum)).astype(BF16)
        att = jnp.where(causal, lax.dot_general(q_in, k_in, nt, preferred_element_type=F32), 0.0)
        att = att + jnp.where(band, lax.dot_general(q_in, k_out, nt, preferred_element_type=F32), 0.0)
        o_intra.append(jnp.dot(att.astype(BF16), v, preferred_element_type=F32))
        vs.append(v)
        q_far.append((q * jnp.exp(cum + jnp.where(is_b, partner, 0.0))).astype(BF16))
        k_far.append((kk * jnp.exp(tot - cum + jnp.where(is_b, 0.0, partner))).astype(BF16))
        e_pair.append(jnp.exp(tot + partner))

    upd = [[lax.dot_general(vs[hh][p * pair:(p + 1) * pair], k_far[hh][p * pair:(p + 1) * pair], tn,
                            preferred_element_type=F32) for p in range(npair)] for hh in heads]
    entering = [[None] * npair for _ in heads]
    for hh in heads:
        st = state_ref[hh]
        for p in order:
            entering[hh][p] = st.astype(BF16)
            st = st * e_pair[hh][p * pair:p * pair + 1, :] + upd[hh][p]
        state_ref[hh] = st
    for hh in heads:
        outs = [lax.dot_general(q_far[hh][p * pair:(p + 1) * pair], entering[hh][p], nt,
                                preferred_element_type=F32) for p in range(npair)]
        o_ref[0, :, hh * dk:(hh + 1) * dk] = (o_intra[hh] + jnp.concatenate(outs, axis=0)).astype(o_ref.dtype)


HG_HEADS_PER_STEP = 2


def _hg_scan(q, f, v, lb, *, reverse):
    b, s, wdt = q.shape
    heads = wdt // HG_KDIM
    hps = HG_HEADS_PER_STEP
    r = _tile(s, 256)
    nb = s // r

    def ridx(c):
        return nb - 1 - c if reverse else c

    blk = pl.BlockSpec((1, r, hps * HG_KDIM), lambda i, h, c: (i, ridx(c), h))
    return pl.pallas_call(
        functools.partial(_hg_scan_body, reverse=reverse, chunk=HG_CHUNK),
        grid=(b, heads // hps, nb),
        in_specs=[blk, blk, blk, pl.BlockSpec((hps, 1, HG_KDIM), lambda i, h, c: (h, 0, 0))],
        out_specs=blk,
        out_shape=jax.ShapeDtypeStruct((b, s, wdt), BF16),
        scratch_shapes=[pltpu.VMEM((hps, HG_KDIM, HG_KDIM), F32)],
        compiler_params=_params("parallel", "parallel", "arbitrary"),
        name="hg_scan_bwd" if reverse else "hg_scan_fwd",
    )(q, f, v, lb)


def _hg_gate_body(of_ref, ob_ref, gate_ref, g_ref, o_ref):
    o = of_ref[...].astype(F32) + ob_ref[...].astype(F32)
    ms = jnp.mean(o * o, axis=-1, keepdims=True)
    o = o * lax.rsqrt(ms + RMS_EPS) * g_ref[...]
    o_ref[...] = (o * _silu(gate_ref[...].astype(F32))).astype(o_ref.dtype)


def _hg_gate(o_f, o_b, gate, norm_g):
    m, wdt = o_f.shape
    heads = wdt // HG_KDIM
    tm = _tile(m, 1024)
    blk = pl.BlockSpec((tm, HG_KDIM), lambda i, h: (i, h))
    return pl.pallas_call(
        _hg_gate_body,
        grid=(m // tm, heads),
        in_specs=[blk, blk, blk, pl.BlockSpec((1, HG_KDIM), lambda i, h: (0, 0))],
        out_specs=blk,
        out_shape=jax.ShapeDtypeStruct((m, wdt), BF16),
        compiler_params=_params("parallel", "parallel"),
        name="hg_gate",
    )(o_f, o_b, gate, norm_g.reshape(1, HG_KDIM).astype(F32))


def _hgrn2_mixer(h, hn, bsz, seq, w_in, lower_bound, norm_g, w_out):
    m, d = hn.shape
    wdt = w_out.shape[0]
    heads = wdt // HG_KDIM
    w_in = w_in.astype(BF16)
    q = _matmul(hn, w_in, BF16, col0=0, ncols=wdt).reshape(bsz, seq, wdt)
    f_fwd = _matmul(hn, w_in, F32, col0=wdt, ncols=wdt).reshape(bsz, seq, wdt)
    f_bwd = _matmul(hn, w_in, F32, col0=2 * wdt, ncols=wdt).reshape(bsz, seq, wdt)
    v = _matmul(hn, w_in, BF16, col0=3 * wdt, ncols=wdt).reshape(bsz, seq, wdt)
    gate = _matmul(hn, w_in, BF16, col0=4 * wdt, ncols=wdt)
    lb = lower_bound.astype(F32).reshape(heads, 1, HG_KDIM)
    o_f = _hg_scan(q, f_fwd, v, lb, reverse=False)
    o_b = _hg_scan(q, f_bwd, v, lb, reverse=True)
    o = _hg_gate(o_f.reshape(m, wdt), o_b.reshape(m, wdt), gate, norm_g)
    return _matmul(o, w_out.astype(BF16), F32, residual=h)


def _rope_matmul_body(x_ref, w_ref, cos_ref, sin_ref, o_ref, *, n_rope_tiles, hd):
    acc = jnp.dot(x_ref[...], w_ref[...], preferred_element_type=F32)
    j = pl.program_id(1)

    @pl.when(j < n_rope_tiles)
    def _():
        cos = cos_ref[...]
        sin = sin_ref[...]
        for c0 in range(0, acc.shape[1], hd):
            x = acc[:, c0:c0 + hd]
            o_ref[:, c0:c0 + hd] = (x * cos + pltpu.roll(x, hd // 2, axis=1) * sin).astype(o_ref.dtype)

    @pl.when(j >= n_rope_tiles)
    def _():
        o_ref[...] = acc.astype(o_ref.dtype)


def _rope_matmul(x, w, cos_m, sin_m, n_rope_cols):
    m, k = x.shape
    n = w.shape[1]
    tm = _tile(m, 1024)
    tn = _tile(math.gcd(n, n_rope_cols), 1024)
    tab = pl.BlockSpec((tm, AT_HEAD_DIM), lambda i, j: (i, 0))
    return pl.pallas_call(
        functools.partial(_rope_matmul_body, n_rope_tiles=n_rope_cols // tn, hd=AT_HEAD_DIM),
        grid=(m // tm, n // tn),
        in_specs=[pl.BlockSpec((tm, k), lambda i, j: (i, 0)), pl.BlockSpec((k, tn), lambda i, j: (0, j)), tab, tab],
        out_specs=pl.BlockSpec((tm, tn), lambda i, j: (i, j)),
        out_shape=jax.ShapeDtypeStruct((m, n), BF16),
        compiler_params=_params("parallel", "parallel"),
        name="rope_matmul",
    )(x, w, cos_m, sin_m)


def _win_attn_body(q_ref, kp_ref, kc_ref, kn_ref, vp_ref, vc_ref, vn_ref, sink_ref, o_ref, *, rep, hd, nb):
    n = pl.program_id(2)
    blk = q_ref.shape[1]
    kps = kc_ref.shape[2] // hd
    rows = rep * blk
    ri = lax.broadcasted_iota(jnp.int32, (rows, 3 * blk), 0)
    qi = ri % blk + blk
    kj = lax.broadcasted_iota(jnp.int32, (rows, 3 * blk), 1)
    valid = jnp.abs(qi - kj) <= blk
    valid = valid & ((kj >= blk) | (n > 0)) & ((kj < 2 * blk) | (n < nb - 1))
    head_of_row = lax.broadcasted_iota(jnp.int32, (rows, 1), 0) // blk
    for kh in range(kps):
        cols = slice(kh * hd, (kh + 1) * hd)
        kcat = jnp.concatenate([kp_ref[0, :, cols], kc_ref[0, :, cols], kn_ref[0, :, cols]], axis=0)
        vcat = jnp.concatenate([vp_ref[0, :, cols], vc_ref[0, :, cols], vn_ref[0, :, cols]], axis=0)
        sink = jnp.zeros((rows, 1), F32)
        for r in range(rep):
            sink = jnp.where(head_of_row == r, sink_ref[kh, :, r:r + 1], sink)
        q0 = kh * rep * hd
        q = jnp.concatenate([q_ref[0, :, q0 + r * hd:q0 + (r + 1) * hd] for r in range(rep)], axis=0)
        s = lax.dot_general(q, kcat, (((1,), (1,)), ((), ())), preferred_element_type=F32) * (hd ** -0.5)
        s = jnp.where(valid, s, -jnp.inf)
        mx = jnp.maximum(jnp.max(s, axis=-1, keepdims=True), sink)
        p = jnp.exp(s - mx)
        denom = jnp.sum(p, axis=-1, keepdims=True) + jnp.exp(sink - mx)
        o = jnp.dot((p / denom).astype(BF16), vcat, preferred_element_type=F32)
        for r in range(rep):
            o_ref[0, :, q0 + r * hd:q0 + (r + 1) * hd] = o[r * blk:(r + 1) * blk].astype(o_ref.dtype)


AT_KV_HEADS_PER_STEP = 1


def _win_attn(qkv, qw, sink):
    b, s, _ = qkv.shape
    hd = AT_HEAD_DIM
    hkv = AT_KV_HEADS
    kps = AT_KV_HEADS_PER_STEP
    rep = qw // hd // hkv
    k_off = qw // (kps * hd)
    v_off = k_off + hkv // kps
    blk = AT_BLOCK
    nb = s // blk
    qspec = pl.BlockSpec((1, blk, kps * rep * hd), lambda i, g, n: (i, n, g))

    def kv(off, col):
        return pl.BlockSpec((1, blk, kps * hd), lambda i, g, n: (i, jnp.clip(n + off, 0, nb - 1), col + g))

    return pl.pallas_call(
        functools.partial(_win_attn_body, rep=rep, hd=hd, nb=nb),
        grid=(b, hkv // kps, nb),
        in_specs=[qspec, kv(-1, k_off), kv(0, k_off), kv(1, k_off), kv(-1, v_off), kv(0, v_off), kv(1, v_off),
                  pl.BlockSpec((kps, 1, rep), lambda i, g, n: (g, 0, 0))],
        out_specs=qspec,
        out_shape=jax.ShapeDtypeStruct((b, s, qw), BF16),
        compiler_params=_params("parallel", "parallel", "parallel"),
        name="win_attn",
    )(qkv, qkv, qkv, qkv, qkv, qkv, qkv, sink.astype(F32).reshape(hkv, 1, rep))


def _window_gqa_mixer(h, hn, bsz, seq, w_qkv, sink, w_out):
    m, d = hn.shape
    qw = w_out.shape[0]
    kvw = AT_KV_HEADS * AT_HEAD_DIM
    half = AT_HEAD_DIM // 2
    inv = ROPE_THETA ** (-jnp.arange(half, dtype=F32) / half)
    ang = jnp.arange(seq, dtype=F32)[:, None] * inv[None, :]
    cos_m = jnp.tile(jnp.concatenate([jnp.cos(ang), jnp.cos(ang)], axis=1), (bsz, 1))
    sin_m = jnp.tile(jnp.concatenate([-jnp.sin(ang), jnp.sin(ang)], axis=1), (bsz, 1))
    qkv = _rope_matmul(hn, w_qkv.astype(BF16), cos_m, sin_m, qw + kvw).reshape(bsz, seq, qw + 2 * kvw)
    o = _win_attn(qkv, qw, sink)
    return _matmul(o.reshape(m, qw), w_out.astype(BF16), F32, residual=h)


def _cross_attn_body(h_ref, g_ref, wq_ref, kv_ref, wo_ref, o_ref, *, heads, hd):
    hres = h_ref[0]
    ms = jnp.mean(hres * hres, axis=-1, keepdims=True)
    xn = (hres * lax.rsqrt(ms + RMS_EPS) * g_ref[...]).astype(BF16)
    q = jnp.dot(xn, wq_ref[...], preferred_element_type=F32).astype(BF16)
    kv = kv_ref[0]
    tm = hres.shape[0]
    s = jnp.concatenate(
        [lax.dot_general(q[:, a * hd:(a + 1) * hd], kv[:, a * hd:(a + 1) * hd], (((1,), (1,)), ((), ())),
                         preferred_element_type=F32) for a in range(heads)], axis=0) * (hd ** -0.5)
    mx = jnp.max(s, axis=-1, keepdims=True)
    p = jnp.exp(s - mx)
    p = (p / jnp.sum(p, axis=-1, keepdims=True)).astype(BF16)
    o = jnp.concatenate(
        [jnp.dot(p[a * tm:(a + 1) * tm], kv[:, (heads + a) * hd:(heads + a + 1) * hd],
                 preferred_element_type=F32) for a in range(heads)], axis=1).astype(BF16)
    o_ref[0] = hres + jnp.dot(o, wo_ref[...], preferred_element_type=F32)


def _cross_attention(h, mem, bsz, seq, norm_x, norm_m, w_q, w_kv, w_out):
    m, d = h.shape
    mt = mem.shape[1]
    cw = w_q.shape[1]
    memn = _rmsnorm(mem.reshape(bsz * mt, d), norm_m, BF16)
    kv = _matmul(memn, w_kv.astype(BF16), BF16).reshape(bsz, mt, 2 * cw)
    tm = _tile(seq, 256)
    out = pl.pallas_call(
        functools.partial(_cross_attn_body, heads=CA_HEADS, hd=CA_HEAD_DIM),
        grid=(bsz, seq // tm),
        in_specs=[pl.BlockSpec((1, tm, d), lambda i, j: (i, j, 0)),
                  pl.BlockSpec((1, d), lambda i, j: (0, 0)),
                  pl.BlockSpec((d, cw), lambda i, j: (0, 0)),
                  pl.BlockSpec((1, mt, 2 * cw), lambda i, j: (i, 0, 0)),
                  pl.BlockSpec((cw, d), lambda i, j: (0, 0))],
        out_specs=pl.BlockSpec((1, tm, d), lambda i, j: (i, j, 0)),
        out_shape=jax.ShapeDtypeStruct((bsz, seq, d), F32),
        compiler_params=_params("parallel", "parallel"),
        name="cross_attn",
    )(h.reshape(bsz, seq, d), norm_x.reshape(1, d).astype(F32), w_q.astype(BF16), kv, w_out.astype(BF16))
    return out.reshape(m, d)


MOE_ROW_BLOCK = 512
MOE_HIDDEN_TILE = 512
MOE_TOKEN_TILE = 256


HIGH_HALF = -65536
DMA_LOOP_UNROLL = 8


def _pack_pairs(x):
    half = x.shape[1] // 2
    bits = lax.bitcast_convert_type(x.astype(BF16).astype(F32), jnp.int32)
    return lax.shift_right_logical(bits[:, :half], jnp.int32(16)) | bits[:, half:]


def _unpack_pairs(p):
    lo = lax.bitcast_convert_type(lax.shift_left(p, jnp.int32(16)), F32)
    hi = lax.bitcast_convert_type(p & jnp.int32(HIGH_HALF), F32)
    return lo, hi


def _moe_norm_logits_body(h_ref, g_ref, wr_ref, xp_ref, lg_ref):
    x = h_ref[...]
    ms = jnp.mean(x * x, axis=-1, keepdims=True)
    xn = x * lax.rsqrt(ms + RMS_EPS) * g_ref[...]
    xp_ref[...] = _pack_pairs(xn)
    lg_ref[...] = jnp.dot(xn, wr_ref[...], preferred_element_type=F32, precision=lax.Precision.HIGHEST)


def _moe_norm_logits(h, g, w_route):
    m, d = h.shape
    tm = _tile(m, 256)
    return pl.pallas_call(
        _moe_norm_logits_body,
        grid=(m // tm,),
        in_specs=[pl.BlockSpec((tm, d), lambda i: (i, 0)),
                  pl.BlockSpec((1, d), lambda i: (0, 0)),
                  pl.BlockSpec((d, LANES), lambda i: (0, 0))],
        out_specs=[pl.BlockSpec((tm, d // 2), lambda i: (i, 0)), pl.BlockSpec((tm, LANES), lambda i: (i, 0))],
        out_shape=[jax.ShapeDtypeStruct((m, d // 2), jnp.int32), jax.ShapeDtypeStruct((m, LANES), F32)],
        compiler_params=_params("parallel"),
        name="moe_norm_logits",
    )(h, g.reshape(1, d).astype(F32), w_route)


def _row_copy(src_hbm, dst_ref, sem, src_row, dst_row):
    return pltpu.make_async_copy(src_hbm.at[pl.ds(src_row, 1)], dst_ref.at[pl.ds(dst_row, 1)], sem)


def _gather_rows_body(idx_ref, src_hbm, o_ref, sem, *, tm):
    base = pl.program_id(0) * tm

    def start(r8, carry):
        for u in range(DMA_LOOP_UNROLL):
            r = r8 * DMA_LOOP_UNROLL + u
            _row_copy(src_hbm, o_ref, sem, idx_ref[base + r], r).start(priority=u % 2)
        return carry

    def wait(r, carry):
        _row_copy(src_hbm, o_ref, sem, 0, r).wait()
        return carry

    lax.fori_loop(0, tm // DMA_LOOP_UNROLL, start, 0)
    lax.fori_loop(0, tm, wait, 0, unroll=DMA_LOOP_UNROLL)


def _gather_rows(src, idx):
    rows = idx.shape[0]
    d = src.shape[1]
    tm = _tile(rows, 512)
    return pl.pallas_call(
        functools.partial(_gather_rows_body, tm=tm),
        grid_spec=pltpu.PrefetchScalarGridSpec(
            num_scalar_prefetch=1,
            grid=(rows // tm,),
            in_specs=[pl.BlockSpec(memory_space=pl.ANY)],
            out_specs=pl.BlockSpec((tm, d), lambda i, idx_ref: (i, 0)),
            scratch_shapes=[pltpu.SemaphoreType.DMA(())]),
        out_shape=jax.ShapeDtypeStruct((rows, d), src.dtype),
        compiler_params=_params("arbitrary"),
        name="moe_gather",
    )(idx, src)


def _experts_body(be_ref, nused_ref, x_ref, wg_ref, wu_ref, wd_ref, o_ref, *acc, nj):
    i = pl.program_id(0)
    j = pl.program_id(1)
    used = i < nused_ref[0]

    @pl.when(jnp.logical_and(j == nj - 1, jnp.logical_not(used)))
    def _():
        o_ref[...] = jnp.zeros_like(o_ref)

    @pl.when(used)
    def _():
        lo, hi = _unpack_pairs(x_ref[...])
        x = jnp.concatenate([lo.astype(BF16), hi.astype(BF16)], axis=1)
        hid = (_silu(jnp.dot(x, wg_ref[0], preferred_element_type=F32))
               * jnp.dot(x, wu_ref[0], preferred_element_type=F32)).astype(BF16)
        part = jnp.dot(hid, wd_ref[0], preferred_element_type=F32)
        if nj == 1:
            o_ref[...] = _pack_pairs(part)
            return
        acc_ref, = acc

        @pl.when(j == 0)
        def _():
            acc_ref[...] = part

        @pl.when(jnp.logical_and(j > 0, j < nj - 1))
        def _():
            acc_ref[...] += part

        @pl.when(j == nj - 1)
        def _():
            o_ref[...] = _pack_pairs(acc_ref[...] + part)


def _experts(buf, blk_expert, n_used, w_gate, w_up, w_down):
    rows, dh = buf.shape
    d = 2 * dh
    hidden = w_gate.shape[2]
    tm = MOE_ROW_BLOCK
    th = _tile(hidden, MOE_HIDDEN_TILE)
    nj = hidden // th
    nblk = rows // tm

    def xmap(i, j, be, nu):
        return (jnp.minimum(i, nu[0] - 1), 0)

    return pl.pallas_call(
        functools.partial(_experts_body, nj=nj),
        grid_spec=pltpu.PrefetchScalarGridSpec(
            num_scalar_prefetch=2,
            grid=(nblk, nj),
            in_specs=[pl.BlockSpec((tm, dh), xmap),
                      pl.BlockSpec((1, d, th), lambda i, j, be, nu: (be[i], 0, j)),
                      pl.BlockSpec((1, d, th), lambda i, j, be, nu: (be[i], 0, j)),
                      pl.BlockSpec((1, th, d), lambda i, j, be, nu: (be[i], j, 0))],
            out_specs=pl.BlockSpec((tm, dh), lambda i, j, be, nu: (i, 0)),
            scratch_shapes=[pltpu.VMEM((tm, d), F32)] if nj > 1 else []),
        out_shape=jax.ShapeDtypeStruct((rows, dh), jnp.int32),
        compiler_params=_params("arbitrary", "arbitrary"),
        name="moe_experts",
    )(blk_expert, n_used, buf, w_gate, w_up, w_down)


def _combine_body(d0_ref, d1_ref, src_hbm, h_ref, gate_ref, o_ref, rows0, rows1, sem, *, tm):
    base = pl.program_id(0) * tm

    def start(r8, carry):
        for u in range(DMA_LOOP_UNROLL):
            r = r8 * DMA_LOOP_UNROLL + u
            _row_copy(src_hbm, rows0, sem.at[0], d0_ref[base + r], r).start(priority=0)
            _row_copy(src_hbm, rows1, sem.at[1], d1_ref[base + r], r).start(priority=1)
        return carry

    def wait(r, carry):
        _row_copy(src_hbm, rows0, sem.at[0], 0, r).wait()
        _row_copy(src_hbm, rows1, sem.at[1], 0, r).wait()
        return carry

    lax.fori_loop(0, tm // DMA_LOOP_UNROLL, start, 0)
    lax.fori_loop(0, tm, wait, 0, unroll=DMA_LOOP_UNROLL)
    g = gate_ref[...]
    half = rows0.shape[1]
    a_lo, a_hi = _unpack_pairs(rows0[...])
    b_lo, b_hi = _unpack_pairs(rows1[...])
    o_ref[:, :half] = h_ref[:, :half] + a_lo * g[:, 0:1] + b_lo * g[:, 1:2]
    o_ref[:, half:] = h_ref[:, half:] + a_hi * g[:, 0:1] + b_hi * g[:, 1:2]


def _combine(h, out_buf, dest, gates):
    m, d = h.shape
    tm = _tile(m, MOE_TOKEN_TILE)
    return pl.pallas_call(
        functools.partial(_combine_body, tm=tm),
        grid_spec=pltpu.PrefetchScalarGridSpec(
            num_scalar_prefetch=2,
            grid=(m // tm,),
            in_specs=[pl.BlockSpec(memory_space=pl.ANY),
                      pl.BlockSpec((tm, d), lambda i, a, b: (i, 0)),
                      pl.BlockSpec((tm, MOE_TOP_K), lambda i, a, b: (i, 0))],
            out_specs=pl.BlockSpec((tm, d), lambda i, a, b: (i, 0)),
            scratch_shapes=[pltpu.VMEM((tm, d // 2), jnp.int32), pltpu.VMEM((tm, d // 2), jnp.int32),
                            pltpu.SemaphoreType.DMA((2,))]),
        out_shape=jax.ShapeDtypeStruct((m, d), F32),
        compiler_params=_params("arbitrary"),
        name="moe_combine",
    )(dest[:, 0], dest[:, 1], out_buf, h, gates)


def _hier_moe(h, norm_g, w_group, b_group, w_expert, b_expert, w_gate, w_up, w_down):
    n, d = h.shape
    w_route = jnp.concatenate([w_group, w_expert], axis=1).astype(F32)
    w_route = jnp.pad(w_route, ((0, 0), (0, LANES - w_route.shape[1])))
    xp, logits = _moe_norm_logits(h, norm_g, w_route)
    g_logits = logits[:, :MOE_GROUPS] + b_group.astype(F32)
    g_prob = jax.nn.softmax(g_logits, axis=-1)
    g_sel = jnp.argmax(g_logits, axis=-1)
    g_weight = jnp.take_along_axis(g_prob, g_sel[:, None], axis=-1)
    e_logits = (logits[:, MOE_GROUPS:MOE_GROUPS + MOE_EXPERTS] + b_expert.astype(F32)
                ).reshape(n, MOE_GROUPS, MOE_EXPERTS_PER_GROUP)
    e_logits = jnp.take_along_axis(e_logits, g_sel[:, None, None], axis=1)[:, 0]
    top_val, top_idx = lax.top_k(e_logits, MOE_TOP_K)
    gates = g_weight * jax.nn.softmax(top_val, axis=-1)
    experts = (g_sel[:, None] * MOE_EXPERTS_PER_GROUP + top_idx).astype(jnp.int32)

    nk = n * MOE_TOP_K
    tm = MOE_ROW_BLOCK
    e_flat = experts.reshape(nk)
    onehot = (e_flat[:, None] == jnp.arange(MOE_EXPERTS, dtype=jnp.int32)[None, :]).astype(jnp.int32)
    rank = jnp.sum((jnp.cumsum(onehot, axis=0) - onehot) * onehot, axis=1)
    counts = jnp.sum(onehot, axis=0)
    padded = (counts + tm - 1) // tm * tm
    pend = jnp.cumsum(padded)
    pstart = pend - padded
    dest = (pstart[e_flat] + rank).astype(jnp.int32)
    n_blocks = -(-nk // tm) + MOE_EXPERTS
    rows = n_blocks * tm
    src = jnp.zeros((rows,), jnp.int32).at[dest].set(jnp.arange(nk, dtype=jnp.int32) // MOE_TOP_K)
    blk_expert = jnp.minimum(jnp.searchsorted(pend, jnp.arange(n_blocks, dtype=jnp.int32) * tm, side='right'),
                             MOE_EXPERTS - 1).astype(jnp.int32)
    n_used = (pend[-1] // tm).astype(jnp.int32).reshape(1)

    buf = _gather_rows(xp, src)
    out_buf = _experts(buf, blk_expert, n_used, w_gate.astype(BF16), w_up.astype(BF16), w_down.astype(BF16))
    return _combine(h, out_buf, dest.reshape(n, MOE_TOP_K), gates.astype(F32))


def _trunk(x, mem, p):
    bsz, seq, d = x.shape
    m = bsz * seq
    depth = p['norm_mix'].shape[0]
    lb_soft = jax.nn.softmax(p['hg_lb'].astype(F32), axis=0)
    lower_bounds = jnp.cumsum(lb_soft, axis=0) - lb_soft[0]
    h = x.reshape(m, d)
    for i in range(depth):
        kind, j = i % N_MIXERS, i // N_MIXERS
        hn = _rmsnorm(h, p['norm_mix'][i], BF16)
        if kind == 0:
            h = _ssd_mixer(h, hn, bsz, seq, p['ssd_w_in'][j], p['ssd_conv_w'][j], p['ssd_conv_b'][j],
                           p['ssd_dt_bias'][j], p['ssd_a_log'][j], p['ssd_d'][j], p['ssd_norm'][j],
                           p['ssd_w_out'][j])
        elif kind == 1:
            h = _hgrn2_mixer(h, hn, bsz, seq, p['hg_w_in'][j], lower_bounds[i], p['hg_norm'][j], p['hg_w_out'][j])
        else:
            h = _window_gqa_mixer(h, hn, bsz, seq, p['at_w_qkv'][j], p['at_sink'][j], p['at_w_out'][j])
        h = _cross_attention(h, mem, bsz, seq, p['norm_cross'][i], p['norm_mem'][i],
                             p['ca_w_q'][i], p['ca_w_kv'][i], p['ca_w_out'][i])
        h = _hier_moe(h, p['norm_moe'][i], p['moe_w_group'][i], p['moe_b_group'][i], p['moe_w_expert'][i],
                      p['moe_b_expert'][i], p['moe_w_gate'][i], p['moe_w_up'][i], p['moe_w_down'][i])
    return _rmsnorm(h, p['norm_final'], F32).reshape(bsz, seq, d)


def kernel(x_prompt, x_sample, mem_prompt, mem_sample, norm_mix, norm_cross, norm_mem, norm_moe, norm_final,
           ssd_w_in, ssd_conv_w, ssd_conv_b, ssd_dt_bias, ssd_a_log, ssd_d, ssd_norm, ssd_w_out,
           hg_w_in, hg_lb, hg_norm, hg_w_out, at_w_qkv, at_sink, at_w_out, ca_w_q, ca_w_kv, ca_w_out,
           moe_w_group, moe_b_group, moe_w_expert, moe_b_expert, moe_w_gate, moe_w_up, moe_w_down):
    p = dict(norm_mix=norm_mix, norm_cross=norm_cross, norm_mem=norm_mem, norm_moe=norm_moe,
             norm_final=norm_final, ssd_w_in=ssd_w_in, ssd_conv_w=ssd_conv_w, ssd_conv_b=ssd_conv_b,
             ssd_dt_bias=ssd_dt_bias, ssd_a_log=ssd_a_log, ssd_d=ssd_d, ssd_norm=ssd_norm, ssd_w_out=ssd_w_out,
             hg_w_in=hg_w_in, hg_lb=hg_lb, hg_norm=hg_norm, hg_w_out=hg_w_out,
             at_w_qkv=at_w_qkv, at_sink=at_sink, at_w_out=at_w_out,
             ca_w_q=ca_w_q, ca_w_kv=ca_w_kv, ca_w_out=ca_w_out,
             moe_w_group=moe_w_group, moe_b_group=moe_b_group, moe_w_expert=moe_w_expert,
             moe_b_expert=moe_b_expert, moe_w_gate=moe_w_gate, moe_w_up=moe_w_up, moe_w_down=moe_w_down)
    for name in ('ssd_w_in', 'ssd_w_out', 'hg_w_in', 'hg_w_out', 'at_w_qkv', 'at_w_out', 'ca_w_q', 'ca_w_kv',
                 'ca_w_out', 'moe_w_gate', 'moe_w_up', 'moe_w_down'):
        p[name] = p[name].astype(BF16)
    return (_trunk(x_prompt, mem_prompt, p), _trunk(x_sample, mem_sample, p))
```

```python
import functools
import math

import jax
import jax.numpy as jnp
from jax import lax
from jax.experimental import pallas as pl
from jax.experimental.pallas import tpu as pltpu

F32 = jnp.float32
BF16 = jnp.bfloat16

RMS_EPS = 1e-6
ROPE_THETA = 10000.0
SSD_HEAD_DIM = 64
SSD_GROUPS = 8
SSD_STATE = 128
SSD_CHUNK = 128
HG_KDIM = 128
HG_CHUNK = 16
AT_HEAD_DIM = 128
AT_KV_HEADS = 8
AT_BLOCK = 128
CA_HEADS = 4
CA_HEAD_DIM = 128
MOE_GROUPS = 4
MOE_EXPERTS_PER_GROUP = 4
MOE_EXPERTS = MOE_GROUPS * MOE_EXPERTS_PER_GROUP
MOE_TOP_K = 2
N_MIXERS = 3

V7X_VMEM_LIMIT_BYTES = 56 * 1024 * 1024
LANES = 128


def _params(*sem):
    return pltpu.CompilerParams(dimension_semantics=sem, vmem_limit_bytes=V7X_VMEM_LIMIT_BYTES)


def _tile(n, pref):
    t = min(n, pref)
    while n % t:
        t //= 2
    return t


def _split_dot(m01, x):
    hi = x.astype(BF16)
    lo = (x - hi.astype(F32)).astype(BF16)
    return (jnp.dot(m01, hi, preferred_element_type=F32) + jnp.dot(m01, lo, preferred_element_type=F32))


def _silu(x):
    return x * jax.nn.sigmoid(x)


def _rmsnorm_body(x_ref, g_ref, o_ref):
    x = x_ref[...].astype(F32)
    ms = jnp.mean(x * x, axis=-1, keepdims=True)
    o_ref[...] = (x * lax.rsqrt(ms + RMS_EPS) * g_ref[...]).astype(o_ref.dtype)


def _rmsnorm(x, g, out_dtype):
    m, d = x.shape
    tm = _tile(m, 256)
    return pl.pallas_call(
        _rmsnorm_body,
        grid=(m // tm,),
        in_specs=[pl.BlockSpec((tm, d), lambda i: (i, 0)), pl.BlockSpec((1, d), lambda i: (0, 0))],
        out_specs=pl.BlockSpec((tm, d), lambda i: (i, 0)),
        out_shape=jax.ShapeDtypeStruct((m, d), out_dtype),
        compiler_params=_params("parallel"),
        name="rmsnorm",
    )(x, g.reshape(1, d).astype(F32))


def _matmul_body(*refs, nk, has_res):
    if has_res:
        x_ref, w_ref, r_ref, o_ref = refs[:4]
        scratch = refs[4:]
    else:
        x_ref, w_ref, o_ref = refs[:3]
        r_ref = None
        scratch = refs[3:]
    if nk == 1:
        acc = jnp.dot(x_ref[...], w_ref[0], preferred_element_type=F32)
        if has_res:
            acc = acc + r_ref[...]
        o_ref[...] = acc.astype(o_ref.dtype)
        return
    acc_ref, = scratch
    k = pl.program_id(2)

    @pl.when(k == 0)
    def _():
        acc_ref[...] = jnp.zeros_like(acc_ref)

    acc_ref[...] += jnp.dot(x_ref[...], w_ref[0], preferred_element_type=F32)

    @pl.when(k == nk - 1)
    def _():
        acc = acc_ref[...]
        if has_res:
            acc = acc + r_ref[...]
        o_ref[...] = acc.astype(o_ref.dtype)


def _layered(w):
    if isinstance(w, tuple):
        return w[0].astype(BF16), w[1]
    return w.astype(BF16)[None], 0


def _matmul(x, w, out_dtype, residual=None, col0=0, ncols=None):
    w, layer = _layered(w)
    m, k = x.shape
    n = w.shape[2] - col0 if ncols is None else ncols
    tm = _tile(m, 1024)
    tn = _tile(math.gcd(n, col0) if col0 else n, 1024)
    tk = _tile(k, 4096 if residual is None else 2048)
    nk = k // tk
    j0 = col0 // tn
    has_res = residual is not None
    in_specs = [pl.BlockSpec((tm, tk), lambda i, j, kk: (i, kk)),
                pl.BlockSpec((1, tk, tn), lambda i, j, kk: (layer, kk, j0 + j))]
    args = [x, w]
    if has_res:
        in_specs.append(pl.BlockSpec((tm, tn), lambda i, j, kk: (i, j)))
        args.append(residual)
    return pl.pallas_call(
        functools.partial(_matmul_body, nk=nk, has_res=has_res),
        grid=(m // tm, n // tn, nk),
        in_specs=in_specs,
        out_specs=pl.BlockSpec((tm, tn), lambda i, j, kk: (i, j)),
        out_shape=jax.ShapeDtypeStruct((m, n), out_dtype),
        scratch_shapes=[pltpu.VMEM((tm, tn), F32)] if nk > 1 else [],
        compiler_params=_params("parallel", "parallel", "arbitrary"),
        name="matmul",
    )(*args)


CONV_HALO = 16


def _ssd_conv_body(x_ref, prev_ref, next_ref, w_ref, b_ref, o_ref, ext_ref, *, tr, width):
    pad = width // 2
    ts = x_ref.shape[1]
    j = pl.program_id(1)
    w = w_ref[...]
    bias = b_ref[...]
    ext_ref[pl.ds(0, CONV_HALO), :] = jnp.where(j > 0, prev_ref[0], jnp.zeros_like(prev_ref[0]))
    ext_ref[pl.ds(CONV_HALO, ts), :] = x_ref[0]
    ext_ref[pl.ds(CONV_HALO + ts, CONV_HALO), :] = jnp.where(j < pl.num_programs(1) - 1, next_ref[0],
                                                              jnp.zeros_like(next_ref[0]))

    def chunk(ci, carry):
        r0 = pl.multiple_of(ci * tr, tr)
        for c0 in range(0, x_ref.shape[2], LANES):
            cols = slice(c0, c0 + LANES)
            ext = ext_ref[pl.ds(r0, tr + 2 * CONV_HALO), cols].astype(F32)
            acc = jnp.zeros((tr, LANES), F32) + bias[:, cols]
            for t in range(width):
                lo = CONV_HALO + t - pad
                acc = acc + ext[lo:lo + tr, :] * w[t:t + 1, cols]
            o_ref[0, pl.ds(r0, tr), cols] = _silu(acc).astype(o_ref.dtype)
        return carry

    lax.fori_loop(0, ts // tr, chunk, 0)


def _ssd_conv(xbc, conv_w, conv_b):
    b, s, c = xbc.shape
    width = conv_w.shape[0]
    tc = _tile(c, 1024)
    ts = _tile(s, 512)
    tr = _tile(ts, 128)
    hb = ts // CONV_HALO
    last = s // CONV_HALO - 1
    return pl.pallas_call(
        functools.partial(_ssd_conv_body, tr=tr, width=width),
        grid=(b, s // ts, c // tc),
        in_specs=[pl.BlockSpec((1, ts, tc), lambda i, j, k: (i, j, k)),
                  pl.BlockSpec((1, CONV_HALO, tc), lambda i, j, k: (i, jnp.maximum(j * hb - 1, 0), k)),
                  pl.BlockSpec((1, CONV_HALO, tc), lambda i, j, k: (i, jnp.minimum((j + 1) * hb, last), k)),
                  pl.BlockSpec((width, tc), lambda i, j, k: (0, k)),
                  pl.BlockSpec((1, tc), lambda i, j, k: (0, k))],
        out_specs=pl.BlockSpec((1, ts, tc), lambda i, j, k: (i, j, k)),
        out_shape=jax.ShapeDtypeStruct((b, s, c), BF16),
        scratch_shapes=[pltpu.VMEM((ts + 2 * CONV_HALO, tc), BF16)],
        compiler_params=_params("parallel", "parallel", "parallel"),
        name="ssd_conv",
    )(xbc, xbc, xbc, conv_w.astype(F32), conv_b.reshape(1, c).astype(F32))


def _hi_lo(x, axis):
    hi = x.astype(BF16)
    lo = (x - hi.astype(F32)).astype(BF16)
    return jnp.concatenate([hi, lo], axis=axis)


def _ssd_scan_body(x_ref, b_ref, c_ref, dtg_ref, dta_ref, ag_ref, aa_ref, y_ref, state_ref, *,
                   reverse, heads, hdim):
    l = x_ref.shape[1]
    n = SSD_STATE
    width = heads * hdim
    gps = x_ref.shape[2] // width
    all_heads = dta_ref.shape[1]
    nt = (((1,), (1,)), ((), ()))

    @pl.when(pl.program_id(2) == 0)
    def _():
        state_ref[...] = jnp.zeros_like(state_ref)

    row = lax.broadcasted_iota(jnp.int32, (l, l), 0)
    col = lax.broadcasted_iota(jnp.int32, (l, l), 1)
    causal = (col >= row) if reverse else (col <= row)
    tri = causal.astype(BF16)
    tri2 = jnp.concatenate([tri, tri], axis=1)
    eye = (row == col).astype(BF16)

    la_g2 = _hi_lo(dtg_ref[0] * ag_ref[...], 1)
    cum_g = lax.dot_general(tri2, la_g2, nt, preferred_element_type=F32)
    cum_tg = lax.dot_general(la_g2, tri2, nt, preferred_element_type=F32)

    dt_a = dta_ref[0]
    cum_all2 = _hi_lo(lax.dot_general(tri2, _hi_lo(dt_a * aa_ref[...], 1), nt, preferred_element_type=F32), 1)
    dt_all = lax.dot_general(eye, dt_a.astype(BF16), nt, preferred_element_type=F32).astype(BF16)
    hh = lax.broadcasted_iota(jnp.int32, (all_heads, width), 0)
    cc = lax.broadcasted_iota(jnp.int32, (all_heads, width), 1)
    first = lax.broadcasted_iota(jnp.int32, (l, 2 * hdim), 1) < hdim
    zero = jnp.zeros((l, 2 * hdim), BF16)
    last = 0 if reverse else l - 1

    for gi in range(gps):
        group = pl.program_id(1) * gps + gi
        expand = (hh == group * heads + cc // hdim).astype(BF16)
        cum_x = jnp.dot(cum_all2, jnp.concatenate([expand, expand], axis=0),
                        preferred_element_type=F32)
        dt_x = jnp.dot(dt_all, expand, preferred_element_type=F32)
        tot_x = cum_x[last:last + 1, :]
        cum = cum_g[:, gi * heads:(gi + 1) * heads]
        cum_t = cum_tg[gi * heads:(gi + 1) * heads, :]

        x = x_ref[0, :, gi * width:(gi + 1) * width].astype(F32)
        xd = x * dt_x
        bm = b_ref[0, :, gi * n:(gi + 1) * n]
        cm = c_ref[0, :, gi * n:(gi + 1) * n]
        g = lax.dot_general(cm, bm, nt, preferred_element_type=F32)
        xd_b = xd.astype(BF16)

        def weights(h):
            d = cum[:, h:h + 1] - cum_t[h:h + 1, :]
            return (g * jnp.exp(jnp.where(causal, d, -1e30))).astype(BF16)

        ys = []
        for hp in range(heads // 2):
            pair = xd_b[:, 2 * hp * hdim:(2 * hp + 2) * hdim]
            rhs = jnp.concatenate([jnp.where(first, pair, zero), jnp.where(first, zero, pair)], axis=0)
            lhs = jnp.concatenate([weights(2 * hp), weights(2 * hp + 1)], axis=1)
            ys.append(jnp.dot(lhs, rhs, preferred_element_type=F32))
        y = jnp.concatenate(ys, axis=1)
        state = state_ref[gi]
        y = y + jnp.dot(cm, state.astype(BF16), preferred_element_type=F32) * jnp.exp(cum_x)
        y_ref[0, :, gi * width:(gi + 1) * width] = y.astype(y_ref.dtype)
        xdd = (xd * jnp.exp(tot_x - cum_x)).astype(BF16)
        upd = lax.dot_general(bm, xdd, (((0,), (0,)), ((), ())), preferred_element_type=F32)
        state_ref[gi] = state * jnp.exp(tot_x) + upd


SSD_GROUPS_PER_STEP = 4


def _ssd_scan(xbc, inner, dt_t, a, *, reverse):
    b, s, _ = xbc.shape
    groups = SSD_GROUPS
    gps = SSD_GROUPS_PER_STEP
    n = SSD_STATE
    width = inner // groups
    b_off = inner // (gps * n)
    c_off = b_off + groups // gps
    heads = width // SSD_HEAD_DIM
    l = SSD_CHUNK
    nc = s // l

    def cidx(c):
        return nc - 1 - c if reverse else c

    all_heads = dt_t.shape[1]
    a_col = a.reshape(-1, 1).astype(F32)
    return pl.pallas_call(
        functools.partial(_ssd_scan_body, reverse=reverse, heads=heads, hdim=SSD_HEAD_DIM),
        grid=(b, groups // gps, nc),
        in_specs=[pl.BlockSpec((1, l, gps * width), lambda i, g, c: (i, cidx(c), g)),
                  pl.BlockSpec((1, l, gps * n), lambda i, g, c: (i, cidx(c), b_off + g)),
                  pl.BlockSpec((1, l, gps * n), lambda i, g, c: (i, cidx(c), c_off + g)),
                  pl.BlockSpec((1, gps * heads, l), lambda i, g, c: (i, g, cidx(c))),
                  pl.BlockSpec((1, all_heads, l), lambda i, g, c: (i, 0, cidx(c))),
                  pl.BlockSpec((gps * heads, 1), lambda i, g, c: (g, 0)),
                  pl.BlockSpec((all_heads, 1), lambda i, g, c: (0, 0))],
        out_specs=pl.BlockSpec((1, l, gps * width), lambda i, g, c: (i, cidx(c), g)),
        out_shape=jax.ShapeDtypeStruct((b, s, inner), BF16),
        scratch_shapes=[pltpu.VMEM((gps, n, width), F32)],
        compiler_params=_params("parallel", "parallel", "arbitrary"),
        name="ssd_scan_bwd" if reverse else "ssd_scan_fwd",
    )(xbc, xbc, xbc, dt_t, dt_t, a_col, a_col)


def _ssd_gate_body(yf_ref, yb_ref, x_ref, z_ref, d_ref, g_ref, o_ref):
    x = x_ref[...].astype(F32)
    y = yf_ref[...].astype(F32) + yb_ref[...].astype(F32) + d_ref[...] * x
    y = y * _silu(z_ref[...].astype(F32))
    ms = jnp.mean(y * y, axis=-1, keepdims=True)
    o_ref[...] = (y * lax.rsqrt(ms + RMS_EPS) * g_ref[...]).astype(o_ref.dtype)


def _ssd_gate(yf, yb, xbc, z, d_x, norm_g):
    m, inner = yf.shape
    width = inner // SSD_GROUPS
    tm = _tile(m, 512)
    blk = pl.BlockSpec((tm, width), lambda i, g: (i, g))
    vec = pl.BlockSpec((1, width), lambda i, g: (0, g))
    return pl.pallas_call(
        _ssd_gate_body,
        grid=(m // tm, SSD_GROUPS),
        in_specs=[blk, blk, blk, blk, vec, vec],
        out_specs=blk,
        out_shape=jax.ShapeDtypeStruct((m, inner), BF16),
        compiler_params=_params("parallel", "parallel"),
        name="ssd_gate",
    )(yf, yb, xbc, z, d_x.reshape(1, inner).astype(F32), norm_g.reshape(1, inner).astype(F32))


def _ssd_mixer(h, hn, bsz, seq, w_in, conv_w, conv_b, dt_bias, a_log, d_skip, norm_g, w_out):
    m, d = hn.shape
    inner = norm_g.shape[0]
    heads = inner // SSD_HEAD_DIM
    gn = SSD_GROUPS * SSD_STATE
    conv_dim = inner + 2 * gn
    z = _matmul(hn, w_in, BF16, col0=0, ncols=inner)
    xbc = _matmul(hn, w_in, BF16, col0=inner, ncols=conv_dim)
    dt_raw = _matmul(hn, w_in, F32, col0=inner + conv_dim)
    xbc = _ssd_conv(xbc.reshape(bsz, seq, conv_dim), conv_w, conv_b)
    dt = jax.nn.softplus(dt_raw.reshape(bsz, seq, 2, heads) + dt_bias.astype(F32))
    dt_t = jnp.transpose(dt, (2, 0, 3, 1))
    a = -jnp.exp(a_log.astype(F32))
    y_f = _ssd_scan(xbc, inner, dt_t[0], a[0], reverse=False)
    y_b = _ssd_scan(xbc, inner, dt_t[1], a[1], reverse=True)
    d_x = jnp.repeat(d_skip.astype(F32), SSD_HEAD_DIM)
    y = _ssd_gate(y_f.reshape(m, inner), y_b.reshape(m, inner), xbc.reshape(m, conv_dim), z, d_x, norm_g)
    return _matmul(y, w_out, F32, residual=h)


def _hg_scan_body(q_ref, f_ref, v_ref, lb_ref, o_ref, state_ref, *, reverse, chunk):
    r = q_ref.shape[1]
    nch = r // chunk
    dk = HG_KDIM
    nheads = q_ref.shape[2] // dk
    nt = (((1,), (1,)), ((), ()))
    tn = (((0,), (0,)), ((), ()))

    @pl.when(pl.program_id(2) == 0)
    def _():
        state_ref[...] = jnp.zeros_like(state_ref)

    pair = 2 * chunk
    npair = r // pair
    row = lax.broadcasted_iota(jnp.int32, (r, r), 0)
    col = lax.broadcasted_iota(jnp.int32, (r, r), 1)
    same = (row // chunk) == (col // chunk)
    causal = same & ((col >= row) if reverse else (col <= row))
    m_cum = causal.astype(BF16)
    same_pair = (row // pair) == (col // pair)
    band = same_pair & ((col // chunk > row // chunk) if reverse else (col // chunk < row // chunk))
    rid = lax.broadcasted_iota(jnp.int32, (r, 1), 0)
    even_chunk = (rid // chunk) % 2 == 0
    is_b = even_chunk if reverse else jnp.logical_not(even_chunk)
    last = 0 if reverse else chunk - 1
    order = range(npair - 1, -1, -1) if reverse else range(npair)

    def per_row_chunk_total(cum):
        c3 = cum.reshape(nch, chunk, dk)
        return jnp.broadcast_to(c3[:, last:last + 1, :], (nch, chunk, dk)).reshape(r, dk)

    heads = range(nheads)
    vs, q_far, k_far, e_pair, o_intra = [], [], [], [], []
    for hh in heads:
        cols = slice(hh * dk, (hh + 1) * dk)
        lb = lb_ref[hh]
        f = lb + (1.0 - lb) * jax.nn.sigmoid(f_ref[0, :, cols])
        logf = jnp.log(f)
        kk = 1.0 - f
        q = _silu(q_ref[0, :, cols].astype(F32))
        v = v_ref[0, :, cols]
        cum = _split_dot(m_cum, logf)
        tot = per_row_chunk_total(cum)
        partner = jnp.where(even_chunk, jnp.concatenate([tot[chunk:], tot[:chunk]], axis=0),
                            jnp.concatenate([tot[-chunk:], tot[:-chunk]], axis=0))
        q_in = (q * jnp.exp(cum)).astype(BF16)
        k_in = (kk * jnp.exp(-cum)).astype(BF16)
        k_out = (kk * jnp.exp(tot - cum)).astype(BF16)
        att = jnp.where(causal, lax.dot_general(q_in, k_in, nt, preferred_element_type=F32), 0.0)
        att = att + jnp.where(band, lax.dot_general(q_in, k_out, nt, preferred_element_type=F32), 0.0)
        o_intra.append(jnp.dot(att.astype(BF16), v, preferred_element_type=F32))
        vs.append(v)
        q_far.append((q * jnp.exp(cum + jnp.where(is_b, partner, 0.0))).astype(BF16))
        k_far.append((kk * jnp.exp(tot - cum + jnp.where(is_b, 0.0, partner))).astype(BF16))
        e_pair.append(jnp.exp(tot + partner))

    upd = [[lax.dot_general(vs[hh][p * pair:(p + 1) * pair], k_far[hh][p * pair:(p + 1) * pair], tn,
                            preferred_element_type=F32) for p in range(npair)] for hh in heads]
    entering = [[None] * npair for _ in heads]
    for hh in heads:
        st = state_ref[hh]
        for p in order:
            entering[hh][p] = st.astype(BF16)
            st = st * e_pair[hh][p * pair:p * pair + 1, :] + upd[hh][p]
        state_ref[hh] = st
    for hh in heads:
        outs = [lax.dot_general(q_far[hh][p * pair:(p + 1) * pair], entering[hh][p], nt,
                                preferred_element_type=F32) for p in range(npair)]
        o_ref[0, :, hh * dk:(hh + 1) * dk] = (o_intra[hh] + jnp.concatenate(outs, axis=0)).astype(o_ref.dtype)


HG_HEADS_PER_STEP = 2


def _hg_scan(q, f, v, lb, *, reverse):
    b, s, wdt = q.shape
    heads = wdt // HG_KDIM
    hps = HG_HEADS_PER_STEP
    r = _tile(s, 256)
    nb = s // r

    def ridx(c):
        return nb - 1 - c if reverse else c

    blk = pl.BlockSpec((1, r, hps * HG_KDIM), lambda i, h, c: (i, ridx(c), h))
    return pl.pallas_call(
        functools.partial(_hg_scan_body, reverse=reverse, chunk=HG_CHUNK),
        grid=(b, heads // hps, nb),
        in_specs=[blk, blk, blk, pl.BlockSpec((hps, 1, HG_KDIM), lambda i, h, c: (h, 0, 0))],
        out_specs=blk,
        out_shape=jax.ShapeDtypeStruct((b, s, wdt), BF16),
        scratch_shapes=[pltpu.VMEM((hps, HG_KDIM, HG_KDIM), F32)],
        compiler_params=_params("parallel", "parallel", "arbitrary"),
        name="hg_scan_bwd" if reverse else "hg_scan_fwd",
    )(q, f, v, lb)


def _hg_gate_body(of_ref, ob_ref, gate_ref, g_ref, o_ref):
    o = of_ref[...].astype(F32) + ob_ref[...].astype(F32)
    ms = jnp.mean(o * o, axis=-1, keepdims=True)
    o = o * lax.rsqrt(ms + RMS_EPS) * g_ref[...]
    o_ref[...] = (o * _silu(gate_ref[...].astype(F32))).astype(o_ref.dtype)


def _hg_gate(o_f, o_b, gate, norm_g):
    m, wdt = o_f.shape
    heads = wdt // HG_KDIM
    tm = _tile(m, 1024)
    blk = pl.BlockSpec((tm, HG_KDIM), lambda i, h: (i, h))
    return pl.pallas_call(
        _hg_gate_body,
        grid=(m // tm, heads),
        in_specs=[blk, blk, blk, pl.BlockSpec((1, HG_KDIM), lambda i, h: (0, 0))],
        out_specs=blk,
        out_shape=jax.ShapeDtypeStruct((m, wdt), BF16),
        compiler_params=_params("parallel", "parallel"),
        name="hg_gate",
    )(o_f, o_b, gate, norm_g.reshape(1, HG_KDIM).astype(F32))


def _hgrn2_mixer(h, hn, bsz, seq, w_in, lower_bound, norm_g, w_out):
    m, d = hn.shape
    wdt = d
    heads = wdt // HG_KDIM
    q = _matmul(hn, w_in, BF16, col0=0, ncols=wdt).reshape(bsz, seq, wdt)
    f_fwd = _matmul(hn, w_in, F32, col0=wdt, ncols=wdt).reshape(bsz, seq, wdt)
    f_bwd = _matmul(hn, w_in, F32, col0=2 * wdt, ncols=wdt).reshape(bsz, seq, wdt)
    v = _matmul(hn, w_in, BF16, col0=3 * wdt, ncols=wdt).reshape(bsz, seq, wdt)
    gate = _matmul(hn, w_in, BF16, col0=4 * wdt, ncols=wdt)
    lb = lower_bound.astype(F32).reshape(heads, 1, HG_KDIM)
    o_f = _hg_scan(q, f_fwd, v, lb, reverse=False)
    o_b = _hg_scan(q, f_bwd, v, lb, reverse=True)
    o = _hg_gate(o_f.reshape(m, wdt), o_b.reshape(m, wdt), gate, norm_g)
    return _matmul(o, w_out, F32, residual=h)


def _rope_matmul_body(x_ref, w_ref, cos_ref, sin_ref, o_ref, *, n_rope_tiles, hd):
    acc = jnp.dot(x_ref[...], w_ref[0], preferred_element_type=F32)
    j = pl.program_id(1)

    @pl.when(j < n_rope_tiles)
    def _():
        cos = cos_ref[...]
        sin = sin_ref[...]
        for c0 in range(0, acc.shape[1], hd):
            x = acc[:, c0:c0 + hd]
            o_ref[:, c0:c0 + hd] = (x * cos + pltpu.roll(x, hd // 2, axis=1) * sin).astype(o_ref.dtype)

    @pl.when(j >= n_rope_tiles)
    def _():
        o_ref[...] = acc.astype(o_ref.dtype)


def _rope_matmul(x, w, cos_m, sin_m, n_rope_cols):
    w, layer = _layered(w)
    m, k = x.shape
    n = w.shape[2]
    tm = _tile(m, 1024)
    tn = _tile(math.gcd(n, n_rope_cols), 1024)
    tab = pl.BlockSpec((tm, AT_HEAD_DIM), lambda i, j: (i, 0))
    return pl.pallas_call(
        functools.partial(_rope_matmul_body, n_rope_tiles=n_rope_cols // tn, hd=AT_HEAD_DIM),
        grid=(m // tm, n // tn),
        in_specs=[pl.BlockSpec((tm, k), lambda i, j: (i, 0)),
                  pl.BlockSpec((1, k, tn), lambda i, j: (layer, 0, j)), tab, tab],
        out_specs=pl.BlockSpec((tm, tn), lambda i, j: (i, j)),
        out_shape=jax.ShapeDtypeStruct((m, n), BF16),
        compiler_params=_params("parallel", "parallel"),
        name="rope_matmul",
    )(x, w, cos_m, sin_m)


def _win_attn_body(q_ref, kp_ref, kc_ref, kn_ref, vp_ref, vc_ref, vn_ref, sink_ref, o_ref, *, rep, hd, nb):
    n = pl.program_id(2)
    blk = q_ref.shape[1]
    kps = kc_ref.shape[2] // hd
    rows = rep * blk
    ri = lax.broadcasted_iota(jnp.int32, (rows, 3 * blk), 0)
    qi = ri % blk + blk
    kj = lax.broadcasted_iota(jnp.int32, (rows, 3 * blk), 1)
    valid = jnp.abs(qi - kj) <= blk
    valid = valid & ((kj >= blk) | (n > 0)) & ((kj < 2 * blk) | (n < nb - 1))
    head_of_row = lax.broadcasted_iota(jnp.int32, (rows, 1), 0) // blk
    for kh in range(kps):
        cols = slice(kh * hd, (kh + 1) * hd)
        kcat = jnp.concatenate([kp_ref[0, :, cols], kc_ref[0, :, cols], kn_ref[0, :, cols]], axis=0)
        vcat = jnp.concatenate([vp_ref[0, :, cols], vc_ref[0, :, cols], vn_ref[0, :, cols]], axis=0)
        sink = jnp.zeros((rows, 1), F32)
        for r in range(rep):
            sink = jnp.where(head_of_row == r, sink_ref[kh, :, r:r + 1], sink)
        q0 = kh * rep * hd
        q = jnp.concatenate([q_ref[0, :, q0 + r * hd:q0 + (r + 1) * hd] for r in range(rep)], axis=0)
        s = lax.dot_general(q, kcat, (((1,), (1,)), ((), ())), preferred_element_type=F32) * (hd ** -0.5)
        s = jnp.where(valid, s, -jnp.inf)
        mx = jnp.maximum(jnp.max(s, axis=-1, keepdims=True), sink)
        p = jnp.exp(s - mx)
        denom = jnp.sum(p, axis=-1, keepdims=True) + jnp.exp(sink - mx)
        o = jnp.dot((p / denom).astype(BF16), vcat, preferred_element_type=F32)
        for r in range(rep):
            o_ref[0, :, q0 + r * hd:q0 + (r + 1) * hd] = o[r * blk:(r + 1) * blk].astype(o_ref.dtype)


AT_KV_HEADS_PER_STEP = 1


def _win_attn(qkv, qw, sink):
    b, s, _ = qkv.shape
    hd = AT_HEAD_DIM
    hkv = AT_KV_HEADS
    kps = AT_KV_HEADS_PER_STEP
    rep = qw // hd // hkv
    k_off = qw // (kps * hd)
    v_off = k_off + hkv // kps
    blk = AT_BLOCK
    nb = s // blk
    qspec = pl.BlockSpec((1, blk, kps * rep * hd), lambda i, g, n: (i, n, g))

    def kv(off, col):
        return pl.BlockSpec((1, blk, kps * hd), lambda i, g, n: (i, jnp.clip(n + off, 0, nb - 1), col + g))

    return pl.pallas_call(
        functools.partial(_win_attn_body, rep=rep, hd=hd, nb=nb),
        grid=(b, hkv // kps, nb),
        in_specs=[qspec, kv(-1, k_off), kv(0, k_off), kv(1, k_off), kv(-1, v_off), kv(0, v_off), kv(1, v_off),
                  pl.BlockSpec((kps, 1, rep), lambda i, g, n: (g, 0, 0))],
        out_specs=qspec,
        out_shape=jax.ShapeDtypeStruct((b, s, qw), BF16),
        compiler_params=_params("parallel", "parallel", "parallel"),
        name="win_attn",
    )(qkv, qkv, qkv, qkv, qkv, qkv, qkv, sink.astype(F32).reshape(hkv, 1, rep))


def _window_gqa_mixer(h, hn, bsz, seq, w_qkv, sink, w_out):
    m, d = hn.shape
    qw = d
    kvw = AT_KV_HEADS * AT_HEAD_DIM
    half = AT_HEAD_DIM // 2
    inv = ROPE_THETA ** (-jnp.arange(half, dtype=F32) / half)
    ang = jnp.arange(seq, dtype=F32)[:, None] * inv[None, :]
    cos_m = jnp.tile(jnp.concatenate([jnp.cos(ang), jnp.cos(ang)], axis=1), (bsz, 1))
    sin_m = jnp.tile(jnp.concatenate([-jnp.sin(ang), jnp.sin(ang)], axis=1), (bsz, 1))
    qkv = _rope_matmul(hn, w_qkv, cos_m, sin_m, qw + kvw).reshape(bsz, seq, qw + 2 * kvw)
    o = _win_attn(qkv, qw, sink)
    return _matmul(o.reshape(m, qw), w_out, F32, residual=h)


def _cross_attn_body(h_ref, g_ref, wq_ref, kv_ref, wo_ref, o_ref, *, heads, hd):
    hres = h_ref[0]
    ms = jnp.mean(hres * hres, axis=-1, keepdims=True)
    xn = (hres * lax.rsqrt(ms + RMS_EPS) * g_ref[...]).astype(BF16)
    q = jnp.dot(xn, wq_ref[...], preferred_element_type=F32).astype(BF16)
    kv = kv_ref[0]
    tm = hres.shape[0]
    s = jnp.concatenate(
        [lax.dot_general(q[:, a * hd:(a + 1) * hd], kv[:, a * hd:(a + 1) * hd], (((1,), (1,)), ((), ())),
                         preferred_element_type=F32) for a in range(heads)], axis=0) * (hd ** -0.5)
    mx = jnp.max(s, axis=-1, keepdims=True)
    p = jnp.exp(s - mx)
    p = (p / jnp.sum(p, axis=-1, keepdims=True)).astype(BF16)
    o = jnp.concatenate(
        [jnp.dot(p[a * tm:(a + 1) * tm], kv[:, (heads + a) * hd:(heads + a + 1) * hd],
                 preferred_element_type=F32) for a in range(heads)], axis=1).astype(BF16)
    o_ref[0] = hres + jnp.dot(o, wo_ref[...], preferred_element_type=F32)


def _cross_attention(h, mem, bsz, seq, norm_x, norm_m, w_q, w_kv, w_out):
    m, d = h.shape
    mt = mem.shape[1]
    cw = w_q.shape[1]
    memn = _rmsnorm(mem.reshape(bsz * mt, d), norm_m, BF16)
    kv = _matmul(memn, w_kv.astype(BF16), BF16).reshape(bsz, mt, 2 * cw)
    tm = _tile(seq, 256)
    out = pl.pallas_call(
        functools.partial(_cross_attn_body, heads=CA_HEADS, hd=CA_HEAD_DIM),
        grid=(bsz, seq // tm),
        in_specs=[pl.BlockSpec((1, tm, d), lambda i, j: (i, j, 0)),
                  pl.BlockSpec((1, d), lambda i, j: (0, 0)),
                  pl.BlockSpec((d, cw), lambda i, j: (0, 0)),
                  pl.BlockSpec((1, mt, 2 * cw), lambda i, j: (i, 0, 0)),
                  pl.BlockSpec((cw, d), lambda i, j: (0, 0))],
        out_specs=pl.BlockSpec((1, tm, d), lambda i, j: (i, j, 0)),
        out_shape=jax.ShapeDtypeStruct((bsz, seq, d), F32),
        compiler_params=_params("parallel", "parallel"),
        name="cross_attn",
    )(h.reshape(bsz, seq, d), norm_x.reshape(1, d).astype(F32), w_q.astype(BF16), kv, w_out.astype(BF16))
    return out.reshape(m, d)


MOE_ROW_BLOCK = 512
MOE_HIDDEN_TILE = 512
MOE_TOKEN_TILE = 256


HIGH_HALF = -65536
DMA_LOOP_UNROLL = 8


def _pack_pairs(x):
    half = x.shape[1] // 2
    bits = lax.bitcast_convert_type(x.astype(BF16).astype(F32), jnp.int32)
    return lax.shift_right_logical(bits[:, :half], jnp.int32(16)) | bits[:, half:]


def _unpack_pairs(p):
    lo = lax.bitcast_convert_type(lax.shift_left(p, jnp.int32(16)), F32)
    hi = lax.bitcast_convert_type(p & jnp.int32(HIGH_HALF), F32)
    return lo, hi


def _store_tokens(ref, packed):
    tm, w = packed.shape
    r = w // LANES
    for s in range(r):
        ref[pl.ds(s, tm, stride=r), :] = packed[:, s * LANES:(s + 1) * LANES]


def _load_tokens(ref, tm):
    r = ref.shape[0] // tm
    return jnp.concatenate([ref[pl.ds(s, tm, stride=r), :] for s in range(r)], axis=1)


def _moe_norm_logits_body(h_ref, g_ref, wr_ref, xp_ref, lg_ref):
    x = h_ref[...]
    ms = jnp.mean(x * x, axis=-1, keepdims=True)
    xn = x * lax.rsqrt(ms + RMS_EPS) * g_ref[...]
    _store_tokens(xp_ref, _pack_pairs(xn))
    lg_ref[...] = jnp.dot(xn, wr_ref[...], preferred_element_type=F32, precision=lax.Precision.HIGHEST)


def _moe_norm_logits(h, g, w_route):
    m, d = h.shape
    tm = _tile(m, 256)
    r = d // 2 // LANES
    return pl.pallas_call(
        _moe_norm_logits_body,
        grid=(m // tm,),
        in_specs=[pl.BlockSpec((tm, d), lambda i: (i, 0)),
                  pl.BlockSpec((1, d), lambda i: (0, 0)),
                  pl.BlockSpec((d, LANES), lambda i: (0, 0))],
        out_specs=[pl.BlockSpec((tm * r, LANES), lambda i: (i, 0)), pl.BlockSpec((tm, LANES), lambda i: (i, 0))],
        out_shape=[jax.ShapeDtypeStruct((m * r, LANES), jnp.int32), jax.ShapeDtypeStruct((m, LANES), F32)],
        compiler_params=_params("parallel"),
        name="moe_norm_logits",
    )(h, g.reshape(1, d).astype(F32), w_route)


def _row_copy(src_hbm, dst_ref, sem, src_tok, dst_tok, r):
    src0 = pl.multiple_of(src_tok * r, r)
    dst0 = pl.multiple_of(dst_tok * r, r)
    return pltpu.make_async_copy(src_hbm.at[pl.ds(src0, r)], dst_ref.at[pl.ds(dst0, r)], sem)


def _gather_rows_body(idx_ref, src_hbm, o_ref, sem, *, tm, r):
    base = pl.program_id(0) * tm

    def start(t8, carry):
        for u in range(DMA_LOOP_UNROLL):
            t = t8 * DMA_LOOP_UNROLL + u
            _row_copy(src_hbm, o_ref, sem, idx_ref[base + t], t, r).start(priority=u % 2)
        return carry

    def wait(t, carry):
        _row_copy(src_hbm, o_ref, sem, 0, t, r).wait()
        return carry

    lax.fori_loop(0, tm // DMA_LOOP_UNROLL, start, 0)
    lax.fori_loop(0, tm, wait, 0, unroll=DMA_LOOP_UNROLL)


def _gather_rows(src, idx, r):
    rows = idx.shape[0]
    tm = _tile(rows, 512)
    return pl.pallas_call(
        functools.partial(_gather_rows_body, tm=tm, r=r),
        grid_spec=pltpu.PrefetchScalarGridSpec(
            num_scalar_prefetch=1,
            grid=(rows // tm,),
            in_specs=[pl.BlockSpec(memory_space=pl.ANY)],
            out_specs=pl.BlockSpec((tm * r, LANES), lambda i, idx_ref: (i, 0)),
            scratch_shapes=[pltpu.SemaphoreType.DMA(())]),
        out_shape=jax.ShapeDtypeStruct((rows * r, LANES), src.dtype),
        compiler_params=_params("arbitrary"),
        name="moe_gather",
    )(idx, src)


def _experts_body(be_ref, nused_ref, x_ref, wg_ref, wu_ref, wd_ref, o_ref, *acc, nj, tm):
    i = pl.program_id(0)
    j = pl.program_id(1)
    used = i < nused_ref[0]

    @pl.when(jnp.logical_and(j == nj - 1, jnp.logical_not(used)))
    def _():
        o_ref[...] = jnp.zeros_like(o_ref)

    @pl.when(used)
    def _():
        lo, hi = _unpack_pairs(_load_tokens(x_ref, tm))
        x = jnp.concatenate([lo.astype(BF16), hi.astype(BF16)], axis=1)
        hid = (_silu(jnp.dot(x, wg_ref[0, 0], preferred_element_type=F32))
               * jnp.dot(x, wu_ref[0, 0], preferred_element_type=F32)).astype(BF16)
        part = jnp.dot(hid, wd_ref[0, 0], preferred_element_type=F32)
        if nj == 1:
            _store_tokens(o_ref, _pack_pairs(part))
            return
        acc_ref, = acc

        @pl.when(j == 0)
        def _():
            acc_ref[...] = part

        @pl.when(jnp.logical_and(j > 0, j < nj - 1))
        def _():
            acc_ref[...] += part

        @pl.when(j == nj - 1)
        def _():
            _store_tokens(o_ref, _pack_pairs(acc_ref[...] + part))


def _experts(buf, r, blk_expert, n_used, w_gate, w_up, w_down):
    rows = buf.shape[0] // r
    d = 2 * r * LANES
    (w_gate, layer), (w_up, _), (w_down, _) = _layered(w_gate), _layered(w_up), _layered(w_down)
    hidden = w_gate.shape[3]
    tm = MOE_ROW_BLOCK
    th = _tile(hidden, MOE_HIDDEN_TILE)
    nj = hidden // th
    nblk = rows // tm

    def xmap(i, j, be, nu):
        return (jnp.minimum(i, nu[0] - 1), 0)

    return pl.pallas_call(
        functools.partial(_experts_body, nj=nj, tm=tm),
        grid_spec=pltpu.PrefetchScalarGridSpec(
            num_scalar_prefetch=2,
            grid=(nblk, nj),
            in_specs=[pl.BlockSpec((tm * r, LANES), xmap),
                      pl.BlockSpec((1, 1, d, th), lambda i, j, be, nu: (layer, be[i], 0, j)),
                      pl.BlockSpec((1, 1, d, th), lambda i, j, be, nu: (layer, be[i], 0, j)),
                      pl.BlockSpec((1, 1, th, d), lambda i, j, be, nu: (layer, be[i], j, 0))],
            out_specs=pl.BlockSpec((tm * r, LANES), lambda i, j, be, nu: (i, 0)),
            scratch_shapes=[pltpu.VMEM((tm, d), F32)] if nj > 1 else []),
        out_shape=jax.ShapeDtypeStruct((rows * r, LANES), jnp.int32),
        compiler_params=_params("arbitrary", "arbitrary"),
        name="moe_experts",
    )(blk_expert, n_used, buf, w_gate, w_up, w_down)


def _combine_body(d0_ref, d1_ref, src_hbm, h_ref, gate_ref, o_ref, rows0, rows1, sem, *, tm, r):
    base = pl.program_id(0) * tm

    def start(t8, carry):
        for u in range(DMA_LOOP_UNROLL):
            t = t8 * DMA_LOOP_UNROLL + u
            _row_copy(src_hbm, rows0, sem.at[0], d0_ref[base + t], t, r).start(priority=0)
            _row_copy(src_hbm, rows1, sem.at[1], d1_ref[base + t], t, r).start(priority=1)
        return carry

    def wait(t, carry):
        _row_copy(src_hbm, rows0, sem.at[0], 0, t, r).wait()
        _row_copy(src_hbm, rows1, sem.at[1], 0, t, r).wait()
        return carry

    lax.fori_loop(0, tm // DMA_LOOP_UNROLL, start, 0)
    lax.fori_loop(0, tm, wait, 0, unroll=DMA_LOOP_UNROLL)
    g = gate_ref[...]
    half = r * LANES
    a_lo, a_hi = _unpack_pairs(_load_tokens(rows0, tm))
    b_lo, b_hi = _unpack_pairs(_load_tokens(rows1, tm))
    o_ref[:, :half] = h_ref[:, :half] + a_lo * g[:, 0:1] + b_lo * g[:, 1:2]
    o_ref[:, half:] = h_ref[:, half:] + a_hi * g[:, 0:1] + b_hi * g[:, 1:2]


def _combine(h, out_buf, r, dest, gates):
    m, d = h.shape
    tm = _tile(m, MOE_TOKEN_TILE)
    return pl.pallas_call(
        functools.partial(_combine_body, tm=tm, r=r),
        grid_spec=pltpu.PrefetchScalarGridSpec(
            num_scalar_prefetch=2,
            grid=(m // tm,),
            in_specs=[pl.BlockSpec(memory_space=pl.ANY),
                      pl.BlockSpec((tm, d), lambda i, a, b: (i, 0)),
                      pl.BlockSpec((tm, MOE_TOP_K), lambda i, a, b: (i, 0))],
            out_specs=pl.BlockSpec((tm, d), lambda i, a, b: (i, 0)),
            scratch_shapes=[pltpu.VMEM((tm * r, LANES), jnp.int32), pltpu.VMEM((tm * r, LANES), jnp.int32),
                            pltpu.SemaphoreType.DMA((2,))]),
        out_shape=jax.ShapeDtypeStruct((m, d), F32),
        compiler_params=_params("arbitrary"),
        name="moe_combine",
    )(dest[:, 0], dest[:, 1], out_buf, h, gates)


def _hier_moe(h, norm_g, w_group, b_group, w_expert, b_expert, w_gate, w_up, w_down):
    n, d = h.shape
    w_route = jnp.concatenate([w_group, w_expert], axis=1).astype(F32)
    w_route = jnp.pad(w_route, ((0, 0), (0, LANES - w_route.shape[1])))
    xp, logits = _moe_norm_logits(h, norm_g, w_route)
    g_logits = logits[:, :MOE_GROUPS] + b_group.astype(F32)
    g_prob = jax.nn.softmax(g_logits, axis=-1)
    g_sel = jnp.argmax(g_logits, axis=-1)
    g_weight = jnp.take_along_axis(g_prob, g_sel[:, None], axis=-1)
    e_logits = (logits[:, MOE_GROUPS:MOE_GROUPS + MOE_EXPERTS] + b_expert.astype(F32)
                ).reshape(n, MOE_GROUPS, MOE_EXPERTS_PER_GROUP)
    e_logits = jnp.take_along_axis(e_logits, g_sel[:, None, None], axis=1)[:, 0]
    top_val, top_idx = lax.top_k(e_logits, MOE_TOP_K)
    gates = g_weight * jax.nn.softmax(top_val, axis=-1)
    experts = (g_sel[:, None] * MOE_EXPERTS_PER_GROUP + top_idx).astype(jnp.int32)

    nk = n * MOE_TOP_K
    tm = MOE_ROW_BLOCK
    e_flat = experts.reshape(nk)
    onehot = (e_flat[:, None] == jnp.arange(MOE_EXPERTS, dtype=jnp.int32)[None, :]).astype(jnp.int32)
    rank = jnp.sum((jnp.cumsum(onehot, axis=0) - onehot) * onehot, axis=1)
    counts = jnp.sum(onehot, axis=0)
    padded = (counts + tm - 1) // tm * tm
    pend = jnp.cumsum(padded)
    pstart = pend - padded
    dest = (pstart[e_flat] + rank).astype(jnp.int32)
    n_blocks = -(-nk // tm) + MOE_EXPERTS
    rows = n_blocks * tm
    src = jnp.zeros((rows,), jnp.int32).at[dest].set(jnp.arange(nk, dtype=jnp.int32) // MOE_TOP_K)
    blk_expert = jnp.minimum(jnp.searchsorted(pend, jnp.arange(n_blocks, dtype=jnp.int32) * tm, side='right'),
                             MOE_EXPERTS - 1).astype(jnp.int32)
    n_used = (pend[-1] // tm).astype(jnp.int32).reshape(1)

    r = d // 2 // LANES
    buf = _gather_rows(xp, src, r)
    out_buf = _experts(buf, r, blk_expert, n_used, w_gate, w_up, w_down)
    return _combine(h, out_buf, r, dest.reshape(n, MOE_TOP_K), gates.astype(F32))


def _trunk(x, mem, p):
    bsz, seq, d = x.shape
    m = bsz * seq
    depth = p['norm_mix'].shape[0]
    lb_soft = jax.nn.softmax(p['hg_lb'].astype(F32), axis=0)
    lower_bounds = jnp.cumsum(lb_soft, axis=0) - lb_soft[0]
    h = x.reshape(m, d)
    for i in range(depth):
        kind, j = i % N_MIXERS, i // N_MIXERS
        hn = _rmsnorm(h, p['norm_mix'][i], BF16)
        if kind == 0:
            h = _ssd_mixer(h, hn, bsz, seq, (p['ssd_w_in'], j), p['ssd_conv_w'][j], p['ssd_conv_b'][j],
                           p['ssd_dt_bias'][j], p['ssd_a_log'][j], p['ssd_d'][j], p['ssd_norm'][j],
                           (p['ssd_w_out'], j))
        elif kind == 1:
            h = _hgrn2_mixer(h, hn, bsz, seq, (p['hg_w_in'], j), lower_bounds[i], p['hg_norm'][j],
                             (p['hg_w_out'], j))
        else:
            h = _window_gqa_mixer(h, hn, bsz, seq, (p['at_w_qkv'], j), p['at_sink'][j], (p['at_w_out'], j))
        h = _cross_attention(h, mem, bsz, seq, p['norm_cross'][i], p['norm_mem'][i],
                             p['ca_w_q'][i], p['ca_w_kv'][i], p['ca_w_out'][i])
        h = _hier_moe(h, p['norm_moe'][i], p['moe_w_group'][i], p['moe_b_group'][i], p['moe_w_expert'][i],
                      p['moe_b_expert'][i], (p['moe_w_gate'], i), (p['moe_w_up'], i), (p['moe_w_down'], i))
    return _rmsnorm(h, p['norm_final'], F32).reshape(bsz, seq, d)


def kernel(x_prompt, x_sample, mem_prompt, mem_sample, norm_mix, norm_cross, norm_mem, norm_moe, norm_final,
           ssd_w_in, ssd_conv_w, ssd_conv_b, ssd_dt_bias, ssd_a_log, ssd_d, ssd_norm, ssd_w_out,
           hg_w_in, hg_lb, hg_norm, hg_w_out, at_w_qkv, at_sink, at_w_out, ca_w_q, ca_w_kv, ca_w_out,
           moe_w_group, moe_b_group, moe_w_expert, moe_b_expert, moe_w_gate, moe_w_up, moe_w_down):
    p = dict(norm_mix=norm_mix, norm_cross=norm_cross, norm_mem=norm_mem, norm_moe=norm_moe,
             norm_final=norm_final, ssd_w_in=ssd_w_in, ssd_conv_w=ssd_conv_w, ssd_conv_b=ssd_conv_b,
             ssd_dt_bias=ssd_dt_bias, ssd_a_log=ssd_a_log, ssd_d=ssd_d, ssd_norm=ssd_norm, ssd_w_out=ssd_w_out,
             hg_w_in=hg_w_in, hg_lb=hg_lb, hg_norm=hg_norm, hg_w_out=hg_w_out,
             at_w_qkv=at_w_qkv, at_sink=at_sink, at_w_out=at_w_out,
             ca_w_q=ca_w_q, ca_w_kv=ca_w_kv, ca_w_out=ca_w_out,
             moe_w_group=moe_w_group, moe_b_group=moe_b_group, moe_w_expert=moe_w_expert,
             moe_b_expert=moe_b_expert, moe_w_gate=moe_w_gate, moe_w_up=moe_w_up, moe_w_down=moe_w_down)
    for name in ('ssd_w_in', 'ssd_w_out', 'hg_w_in', 'hg_w_out', 'at_w_qkv', 'at_w_out', 'ca_w_q', 'ca_w_kv',
                 'ca_w_out', 'moe_w_gate', 'moe_w_up', 'moe_w_down'):
        p[name] = p[name].astype(BF16)
    return (_trunk(x_prompt, mem_prompt, p), _trunk(x_sample, mem_sample, p))
```

```python
import functools
import math

import jax
import jax.numpy as jnp
from jax import lax
from jax.experimental import pallas as pl
from jax.experimental.pallas import tpu as pltpu

F32 = jnp.float32
BF16 = jnp.bfloat16

RMS_EPS = 1e-6
ROPE_THETA = 10000.0
SSD_HEAD_DIM = 64
SSD_GROUPS = 8
SSD_STATE = 128
SSD_CHUNK = 128
HG_KDIM = 128
HG_CHUNK = 16
AT_HEAD_DIM = 128
AT_KV_HEADS = 8
AT_BLOCK = 128
CA_HEADS = 4
CA_HEAD_DIM = 128
MOE_GROUPS = 4
MOE_EXPERTS_PER_GROUP = 4
MOE_EXPERTS = MOE_GROUPS * MOE_EXPERTS_PER_GROUP
MOE_TOP_K = 2
N_MIXERS = 3

V7X_VMEM_LIMIT_BYTES = 56 * 1024 * 1024
LANES = 128


def _params(*sem):
    return pltpu.CompilerParams(dimension_semantics=sem, vmem_limit_bytes=V7X_VMEM_LIMIT_BYTES)


def _tile(n, pref):
    t = min(n, pref)
    while n % t:
        t //= 2
    return t


def _split_dot(m01, x):
    hi = x.astype(BF16)
    lo = (x - hi.astype(F32)).astype(BF16)
    return (jnp.dot(m01, hi, preferred_element_type=F32) + jnp.dot(m01, lo, preferred_element_type=F32))


def _silu(x):
    return x * jax.nn.sigmoid(x)


def _rmsnorm_body(x_ref, g_ref, o_ref):
    x = x_ref[...].astype(F32)
    ms = jnp.mean(x * x, axis=-1, keepdims=True)
    o_ref[...] = (x * lax.rsqrt(ms + RMS_EPS) * g_ref[...]).astype(o_ref.dtype)


def _rmsnorm(x, g, out_dtype):
    m, d = x.shape
    tm = _tile(m, 256)
    return pl.pallas_call(
        _rmsnorm_body,
        grid=(m // tm,),
        in_specs=[pl.BlockSpec((tm, d), lambda i: (i, 0)), pl.BlockSpec((1, d), lambda i: (0, 0))],
        out_specs=pl.BlockSpec((tm, d), lambda i: (i, 0)),
        out_shape=jax.ShapeDtypeStruct((m, d), out_dtype),
        compiler_params=_params("parallel"),
        name="rmsnorm",
    )(x, g.reshape(1, d).astype(F32))


def _matmul_body(*refs, nk, has_res):
    if has_res:
        x_ref, w_ref, r_ref, o_ref = refs[:4]
        scratch = refs[4:]
    else:
        x_ref, w_ref, o_ref = refs[:3]
        r_ref = None
        scratch = refs[3:]
    if nk == 1:
        acc = jnp.dot(x_ref[...], w_ref[0], preferred_element_type=F32)
        if has_res:
            acc = acc + r_ref[...]
        o_ref[...] = acc.astype(o_ref.dtype)
        return
    acc_ref, = scratch
    k = pl.program_id(2)

    @pl.when(k == 0)
    def _():
        acc_ref[...] = jnp.zeros_like(acc_ref)

    acc_ref[...] += jnp.dot(x_ref[...], w_ref[0], preferred_element_type=F32)

    @pl.when(k == nk - 1)
    def _():
        acc = acc_ref[...]
        if has_res:
            acc = acc + r_ref[...]
        o_ref[...] = acc.astype(o_ref.dtype)


def _layered(w):
    if isinstance(w, tuple):
        return w[0].astype(BF16), w[1]
    return w.astype(BF16)[None], 0


def _matmul(x, w, out_dtype, residual=None, col0=0, ncols=None):
    w, layer = _layered(w)
    m, k = x.shape
    n = w.shape[2] - col0 if ncols is None else ncols
    tm = _tile(m, 1024)
    tn = _tile(math.gcd(n, col0) if col0 else n, 1024)
    tk = _tile(k, 4096 if residual is None else 2048)
    nk = k // tk
    j0 = col0 // tn
    has_res = residual is not None
    in_specs = [pl.BlockSpec((tm, tk), lambda i, j, kk: (i, kk)),
                pl.BlockSpec((1, tk, tn), lambda i, j, kk: (layer, kk, j0 + j))]
    args = [x, w]
    if has_res:
        in_specs.append(pl.BlockSpec((tm, tn), lambda i, j, kk: (i, j)))
        args.append(residual)
    return pl.pallas_call(
        functools.partial(_matmul_body, nk=nk, has_res=has_res),
        grid=(m // tm, n // tn, nk),
        in_specs=in_specs,
        out_specs=pl.BlockSpec((tm, tn), lambda i, j, kk: (i, j)),
        out_shape=jax.ShapeDtypeStruct((m, n), out_dtype),
        scratch_shapes=[pltpu.VMEM((tm, tn), F32)] if nk > 1 else [],
        compiler_params=_params("parallel", "parallel", "arbitrary"),
        name="matmul",
    )(*args)


CONV_HALO = 16


def _ssd_conv_body(x_ref, prev_ref, next_ref, w_ref, b_ref, o_ref, ext_ref, *, tr, width):
    pad = width // 2
    ts = x_ref.shape[1]
    j = pl.program_id(1)
    w = w_ref[...]
    bias = b_ref[...]
    ext_ref[pl.ds(0, CONV_HALO), :] = jnp.where(j > 0, prev_ref[0], jnp.zeros_like(prev_ref[0]))
    ext_ref[pl.ds(CONV_HALO, ts), :] = x_ref[0]
    ext_ref[pl.ds(CONV_HALO + ts, CONV_HALO), :] = jnp.where(j < pl.num_programs(1) - 1, next_ref[0],
                                                              jnp.zeros_like(next_ref[0]))

    def chunk(ci, carry):
        r0 = pl.multiple_of(ci * tr, tr)
        for c0 in range(0, x_ref.shape[2], LANES):
            cols = slice(c0, c0 + LANES)
            ext = ext_ref[pl.ds(r0, tr + 2 * CONV_HALO), cols].astype(F32)
            acc = jnp.zeros((tr, LANES), F32) + bias[:, cols]
            for t in range(width):
                lo = CONV_HALO + t - pad
                acc = acc + ext[lo:lo + tr, :] * w[t:t + 1, cols]
            o_ref[0, pl.ds(r0, tr), cols] = _silu(acc).astype(o_ref.dtype)
        return carry

    lax.fori_loop(0, ts // tr, chunk, 0)


def _ssd_conv(xbc, conv_w, conv_b):
    b, s, c = xbc.shape
    width = conv_w.shape[0]
    tc = _tile(c, 1024)
    ts = _tile(s, 512)
    tr = _tile(ts, 128)
    hb = ts // CONV_HALO
    last = s // CONV_HALO - 1
    return pl.pallas_call(
        functools.partial(_ssd_conv_body, tr=tr, width=width),
        grid=(b, s // ts, c // tc),
        in_specs=[pl.BlockSpec((1, ts, tc), lambda i, j, k: (i, j, k)),
                  pl.BlockSpec((1, CONV_HALO, tc), lambda i, j, k: (i, jnp.maximum(j * hb - 1, 0), k)),
                  pl.BlockSpec((1, CONV_HALO, tc), lambda i, j, k: (i, jnp.minimum((j + 1) * hb, last), k)),
                  pl.BlockSpec((width, tc), lambda i, j, k: (0, k)),
                  pl.BlockSpec((1, tc), lambda i, j, k: (0, k))],
        out_specs=pl.BlockSpec((1, ts, tc), lambda i, j, k: (i, j, k)),
        out_shape=jax.ShapeDtypeStruct((b, s, c), BF16),
        scratch_shapes=[pltpu.VMEM((ts + 2 * CONV_HALO, tc), BF16)],
        compiler_params=_params("parallel", "parallel", "parallel"),
        name="ssd_conv",
    )(xbc, xbc, xbc, conv_w.astype(F32), conv_b.reshape(1, c).astype(F32))


def _hi_lo(x, axis):
    hi = x.astype(BF16)
    lo = (x - hi.astype(F32)).astype(BF16)
    return jnp.concatenate([hi, lo], axis=axis)


def _ssd_scan_body(x_ref, b_ref, c_ref, dtg_ref, dta_ref, ag_ref, aa_ref, y_ref, state_ref, *,
                   reverse, heads, hdim):
    l = x_ref.shape[1]
    n = SSD_STATE
    width = heads * hdim
    gps = x_ref.shape[2] // width
    all_heads = dta_ref.shape[1]
    nt = (((1,), (1,)), ((), ()))

    @pl.when(pl.program_id(2) == 0)
    def _():
        state_ref[...] = jnp.zeros_like(state_ref)

    row = lax.broadcasted_iota(jnp.int32, (l, l), 0)
    col = lax.broadcasted_iota(jnp.int32, (l, l), 1)
    causal = (col >= row) if reverse else (col <= row)
    tri = causal.astype(BF16)
    tri2 = jnp.concatenate([tri, tri], axis=1)
    eye = (row == col).astype(BF16)

    la_g2 = _hi_lo(dtg_ref[0] * ag_ref[...], 1)
    cum_g = lax.dot_general(tri2, la_g2, nt, preferred_element_type=F32)
    cum_tg = lax.dot_general(la_g2, tri2, nt, preferred_element_type=F32)

    dt_a = dta_ref[0]
    cum_all2 = _hi_lo(lax.dot_general(tri2, _hi_lo(dt_a * aa_ref[...], 1), nt, preferred_element_type=F32), 1)
    dt_all = lax.dot_general(eye, dt_a.astype(BF16), nt, preferred_element_type=F32).astype(BF16)
    hh = lax.broadcasted_iota(jnp.int32, (all_heads, width), 0)
    cc = lax.broadcasted_iota(jnp.int32, (all_heads, width), 1)
    first = lax.broadcasted_iota(jnp.int32, (l, 2 * hdim), 1) < hdim
    zero = jnp.zeros((l, 2 * hdim), BF16)
    last = 0 if reverse else l - 1

    for gi in range(gps):
        group = pl.program_id(1) * gps + gi
        expand = (hh == group * heads + cc // hdim).astype(BF16)
        cum_x = jnp.dot(cum_all2, jnp.concatenate([expand, expand], axis=0),
                        preferred_element_type=F32)
        dt_x = jnp.dot(dt_all, expand, preferred_element_type=F32)
        tot_x = cum_x[last:last + 1, :]
        cum = cum_g[:, gi * heads:(gi + 1) * heads]
        cum_t = cum_tg[gi * heads:(gi + 1) * heads, :]

        x = x_ref[0, :, gi * width:(gi + 1) * width].astype(F32)
        xd = x * dt_x
        bm = b_ref[0, :, gi * n:(gi + 1) * n]
        cm = c_ref[0, :, gi * n:(gi + 1) * n]
        g = lax.dot_general(cm, bm, nt, preferred_element_type=F32)
        xd_b = xd.astype(BF16)

        def weights(h):
            d = cum[:, h:h + 1] - cum_t[h:h + 1, :]
            return (g * jnp.exp(jnp.where(causal, d, -1e30))).astype(BF16)

        ys = []
        for hp in range(heads // 2):
            pair = xd_b[:, 2 * hp * hdim:(2 * hp + 2) * hdim]
            rhs = jnp.concatenate([jnp.where(first, pair, zero), jnp.where(first, zero, pair)], axis=0)
            lhs = jnp.concatenate([weights(2 * hp), weights(2 * hp + 1)], axis=1)
            ys.append(jnp.dot(lhs, rhs, preferred_element_type=F32))
        y = jnp.concatenate(ys, axis=1)
        state = state_ref[gi]
        y = y + jnp.dot(cm, state.astype(BF16), preferred_element_type=F32) * jnp.exp(cum_x)
        y_ref[0, :, gi * width:(gi + 1) * width] = y.astype(y_ref.dtype)
        xdd = (xd * jnp.exp(tot_x - cum_x)).astype(BF16)
        upd = lax.dot_general(bm, xdd, (((0,), (0,)), ((), ())), preferred_element_type=F32)
        state_ref[gi] = state * jnp.exp(tot_x) + upd


SSD_GROUPS_PER_STEP = 4


def _ssd_scan(xbc, inner, dt_t, a, *, reverse):
    b, s, _ = xbc.shape
    groups = SSD_GROUPS
    gps = SSD_GROUPS_PER_STEP
    n = SSD_STATE
    width = inner // groups
    b_off = inner // (gps * n)
    c_off = b_off + groups // gps
    heads = width // SSD_HEAD_DIM
    l = SSD_CHUNK
    nc = s // l

    def cidx(c):
        return nc - 1 - c if reverse else c

    all_heads = dt_t.shape[1]
    a_col = a.reshape(-1, 1).astype(F32)
    return pl.pallas_call(
        functools.partial(_ssd_scan_body, reverse=reverse, heads=heads, hdim=SSD_HEAD_DIM),
        grid=(b, groups // gps, nc),
        in_specs=[pl.BlockSpec((1, l, gps * width), lambda i, g, c: (i, cidx(c), g)),
                  pl.BlockSpec((1, l, gps * n), lambda i, g, c: (i, cidx(c), b_off + g)),
                  pl.BlockSpec((1, l, gps * n), lambda i, g, c: (i, cidx(c), c_off + g)),
                  pl.BlockSpec((1, gps * heads, l), lambda i, g, c: (i, g, cidx(c))),
                  pl.BlockSpec((1, all_heads, l), lambda i, g, c: (i, 0, cidx(c))),
                  pl.BlockSpec((gps * heads, 1), lambda i, g, c: (g, 0)),
                  pl.BlockSpec((all_heads, 1), lambda i, g, c: (0, 0))],
        out_specs=pl.BlockSpec((1, l, gps * width), lambda i, g, c: (i, cidx(c), g)),
        out_shape=jax.ShapeDtypeStruct((b, s, inner), BF16),
        scratch_shapes=[pltpu.VMEM((gps, n, width), F32)],
        compiler_params=_params("parallel", "parallel", "arbitrary"),
        name="ssd_scan_bwd" if reverse else "ssd_scan_fwd",
    )(xbc, xbc, xbc, dt_t, dt_t, a_col, a_col)


def _ssd_gate_body(yf_ref, yb_ref, x_ref, z_ref, d_ref, g_ref, o_ref):
    x = x_ref[...].astype(F32)
    y = yf_ref[...].astype(F32) + yb_ref[...].astype(F32) + d_ref[...] * x
    y = y * _silu(z_ref[...].astype(F32))
    ms = jnp.mean(y * y, axis=-1, keepdims=True)
    o_ref[...] = (y * lax.rsqrt(ms + RMS_EPS) * g_ref[...]).astype(o_ref.dtype)


def _ssd_gate(yf, yb, xbc, z, d_x, norm_g):
    m, inner = yf.shape
    width = inner // SSD_GROUPS
    tm = _tile(m, 512)
    blk = pl.BlockSpec((tm, width), lambda i, g: (i, g))
    vec = pl.BlockSpec((1, width), lambda i, g: (0, g))
    return pl.pallas_call(
        _ssd_gate_body,
        grid=(m // tm, SSD_GROUPS),
        in_specs=[blk, blk, blk, blk, vec, vec],
        out_specs=blk,
        out_shape=jax.ShapeDtypeStruct((m, inner), BF16),
        compiler_params=_params("parallel", "parallel"),
        name="ssd_gate",
    )(yf, yb, xbc, z, d_x.reshape(1, inner).astype(F32), norm_g.reshape(1, inner).astype(F32))


def _ssd_mixer(h, hn, bsz, seq, w_in, conv_w, conv_b, dt_bias, a_log, d_skip, norm_g, w_out):
    m, d = hn.shape
    inner = norm_g.shape[0]
    heads = inner // SSD_HEAD_DIM
    gn = SSD_GROUPS * SSD_STATE
    conv_dim = inner + 2 * gn
    z = _matmul(hn, w_in, BF16, col0=0, ncols=inner)
    xbc = _matmul(hn, w_in, BF16, col0=inner, ncols=conv_dim)
    dt_raw = _matmul(hn, w_in, F32, col0=inner + conv_dim)
    xbc = _ssd_conv(xbc.reshape(bsz, seq, conv_dim), conv_w, conv_b)
    dt = jax.nn.softplus(dt_raw.reshape(bsz, seq, 2, heads) + dt_bias.astype(F32))
    dt_t = jnp.transpose(dt, (2, 0, 3, 1))
    a = -jnp.exp(a_log.astype(F32))
    y_f = _ssd_scan(xbc, inner, dt_t[0], a[0], reverse=False)
    y_b = _ssd_scan(xbc, inner, dt_t[1], a[1], reverse=True)
    d_x = jnp.repeat(d_skip.astype(F32), SSD_HEAD_DIM)
    y = _ssd_gate(y_f.reshape(m, inner), y_b.reshape(m, inner), xbc.reshape(m, conv_dim), z, d_x, norm_g)
    return _matmul(y, w_out, F32, residual=h)


def _hg_scan_body(q_ref, f_ref, v_ref, lb_ref, o_ref, state_ref, *, reverse, chunk):
    r = q_ref.shape[1]
    nch = r // chunk
    dk = HG_KDIM
    nheads = q_ref.shape[2] // dk
    nt = (((1,), (1,)), ((), ()))
    tn = (((0,), (0,)), ((), ()))

    @pl.when(pl.program_id(2) == 0)
    def _():
        state_ref[...] = jnp.zeros_like(state_ref)

    pair = 2 * chunk
    npair = r // pair
    row = lax.broadcasted_iota(jnp.int32, (r, r), 0)
    col = lax.broadcasted_iota(jnp.int32, (r, r), 1)
    same = (row // chunk) == (col // chunk)
    causal = same & ((col >= row) if reverse else (col <= row))
    m_cum = causal.astype(BF16)
    same_pair = (row // pair) == (col // pair)
    band = same_pair & ((col // chunk > row // chunk) if reverse else (col // chunk < row // chunk))
    rid = lax.broadcasted_iota(jnp.int32, (r, 1), 0)
    even_chunk = (rid // chunk) % 2 == 0
    is_b = even_chunk if reverse else jnp.logical_not(even_chunk)
    last = 0 if reverse else chunk - 1
    order = range(npair - 1, -1, -1) if reverse else range(npair)

    def per_row_chunk_total(cum):
        c3 = cum.reshape(nch, chunk, dk)
        return jnp.broadcast_to(c3[:, last:last + 1, :], (nch, chunk, dk)).reshape(r, dk)

    heads = range(nheads)
    vs, q_far, k_far, e_pair, o_intra = [], [], [], [], []
    for hh in heads:
        cols = slice(hh * dk, (hh + 1) * dk)
        lb = lb_ref[hh]
        f = lb + (1.0 - lb) * jax.nn.sigmoid(f_ref[0, :, cols])
        logf = jnp.log(f)
        kk = 1.0 - f
        q = _silu(q_ref[0, :, cols].astype(F32))
        v = v_ref[0, :, cols]
        cum = _split_dot(m_cum, logf)
        tot = per_row_chunk_total(cum)
        partner = jnp.where(even_chunk, jnp.concatenate([tot[chunk:], tot[:chunk]], axis=0),
                            jnp.concatenate([tot[-chunk:], tot[:-chunk]], axis=0))
        q_in = (q * jnp.exp(cum)).astype(BF16)
        k_in = (kk * jnp.exp(-cum)).astype(BF16)
        k_out = (kk * jnp.exp(tot - cum)).astype(BF16)
        att = jnp.where(causal, lax.dot_general(q_in, k_in, nt, preferred_element_type=F32), 0.0)
        att = att + jnp.where(band, lax.dot_general(q_in, k_out, nt, preferred_element_type=F32), 0.0)
        o_intra.append(jnp.dot(att.astype(BF16), v, preferred_element_type=F32))
        vs.append(v)
        q_far.append((q * jnp.exp(cum + jnp.where(is_b, partner, 0.0))).astype(BF16))
        k_far.append((kk * jnp.exp(tot - cum + jnp.where(is_b, 0.0, partner))).astype(BF16))
        e_pair.append(jnp.exp(tot + partner))

    upd = [[lax.dot_general(vs[hh][p * pair:(p + 1) * pair], k_far[hh][p * pair:(p + 1) * pair], tn,
                            preferred_element_type=F32) for p in range(npair)] for hh in heads]
    entering = [[None] * npair for _ in heads]
    for hh in heads:
        st = state_ref[hh]
        for p in order:
            entering[hh][p] = st.astype(BF16)
            st = st * e_pair[hh][p * pair:p * pair + 1, :] + upd[hh][p]
        state_ref[hh] = st
    for hh in heads:
        outs = [lax.dot_general(q_far[hh][p * pair:(p + 1) * pair], entering[hh][p], nt,
                                preferred_element_type=F32) for p in range(npair)]
        o_ref[0, :, hh * dk:(hh + 1) * dk] = (o_intra[hh] + jnp.concatenate(outs, axis=0)).astype(o_ref.dtype)


HG_HEADS_PER_STEP = 2


def _hg_scan(q, f, v, lb, *, reverse):
    b, s, wdt = q.shape
    heads = wdt // HG_KDIM
    hps = HG_HEADS_PER_STEP
    r = _tile(s, 256)
    nb = s // r

    def ridx(c):
        return nb - 1 - c if reverse else c

    blk = pl.BlockSpec((1, r, hps * HG_KDIM), lambda i, h, c: (i, ridx(c), h))
    return pl.pallas_call(
        functools.partial(_hg_scan_body, reverse=reverse, chunk=HG_CHUNK),
        grid=(b, heads // hps, nb),
        in_specs=[blk, blk, blk, pl.BlockSpec((hps, 1, HG_KDIM), lambda i, h, c: (h, 0, 0))],
        out_specs=blk,
        out_shape=jax.ShapeDtypeStruct((b, s, wdt), BF16),
        scratch_shapes=[pltpu.VMEM((hps, HG_KDIM, HG_KDIM), F32)],
        compiler_params=_params("parallel", "parallel", "arbitrary"),
        name="hg_scan_bwd" if reverse else "hg_scan_fwd",
    )(q, f, v, lb)


def _hg_gate_body(of_ref, ob_ref, gate_ref, g_ref, o_ref):
    o = of_ref[...].astype(F32) + ob_ref[...].astype(F32)
    ms = jnp.mean(o * o, axis=-1, keepdims=True)
    o = o * lax.rsqrt(ms + RMS_EPS) * g_ref[...]
    o_ref[...] = (o * _silu(gate_ref[...].astype(F32))).astype(o_ref.dtype)


def _hg_gate(o_f, o_b, gate, norm_g):
    m, wdt = o_f.shape
    heads = wdt // HG_KDIM
    tm = _tile(m, 1024)
    blk = pl.BlockSpec((tm, HG_KDIM), lambda i, h: (i, h))
    return pl.pallas_call(
        _hg_gate_body,
        grid=(m // tm, heads),
        in_specs=[blk, blk, blk, pl.BlockSpec((1, HG_KDIM), lambda i, h: (0, 0))],
        out_specs=blk,
        out_shape=jax.ShapeDtypeStruct((m, wdt), BF16),
        compiler_params=_params("parallel", "parallel"),
        name="hg_gate",
    )(o_f, o_b, gate, norm_g.reshape(1, HG_KDIM).astype(F32))


def _hgrn2_mixer(h, hn, bsz, seq, w_in, lower_bound, norm_g, w_out):
    m, d = hn.shape
    wdt = d
    heads = wdt // HG_KDIM
    q = _matmul(hn, w_in, BF16, col0=0, ncols=wdt).reshape(bsz, seq, wdt)
    f_fwd = _matmul(hn, w_in, F32, col0=wdt, ncols=wdt).reshape(bsz, seq, wdt)
    f_bwd = _matmul(hn, w_in, F32, col0=2 * wdt, ncols=wdt).reshape(bsz, seq, wdt)
    v = _matmul(hn, w_in, BF16, col0=3 * wdt, ncols=wdt).reshape(bsz, seq, wdt)
    gate = _matmul(hn, w_in, BF16, col0=4 * wdt, ncols=wdt)
    lb = lower_bound.astype(F32).reshape(heads, 1, HG_KDIM)
    o_f = _hg_scan(q, f_fwd, v, lb, reverse=False)
    o_b = _hg_scan(q, f_bwd, v, lb, reverse=True)
    o = _hg_gate(o_f.reshape(m, wdt), o_b.reshape(m, wdt), gate, norm_g)
    return _matmul(o, w_out, F32, residual=h)


def _rope_matmul_body(x_ref, w_ref, cos_ref, sin_ref, o_ref, *, n_rope_tiles, hd):
    acc = jnp.dot(x_ref[...], w_ref[0], preferred_element_type=F32)
    j = pl.program_id(1)

    @pl.when(j < n_rope_tiles)
    def _():
        cos = cos_ref[...]
        sin = sin_ref[...]
        for c0 in range(0, acc.shape[1], hd):
            x = acc[:, c0:c0 + hd]
            o_ref[:, c0:c0 + hd] = (x * cos + pltpu.roll(x, hd // 2, axis=1) * sin).astype(o_ref.dtype)

    @pl.when(j >= n_rope_tiles)
    def _():
        o_ref[...] = acc.astype(o_ref.dtype)


def _rope_matmul(x, w, cos_m, sin_m, n_rope_cols):
    w, layer = _layered(w)
    m, k = x.shape
    n = w.shape[2]
    tm = _tile(m, 1024)
    tn = _tile(math.gcd(n, n_rope_cols), 1024)
    tab = pl.BlockSpec((tm, AT_HEAD_DIM), lambda i, j: (i, 0))
    return pl.pallas_call(
        functools.partial(_rope_matmul_body, n_rope_tiles=n_rope_cols // tn, hd=AT_HEAD_DIM),
        grid=(m // tm, n // tn),
        in_specs=[pl.BlockSpec((tm, k), lambda i, j: (i, 0)),
                  pl.BlockSpec((1, k, tn), lambda i, j: (layer, 0, j)), tab, tab],
        out_specs=pl.BlockSpec((tm, tn), lambda i, j: (i, j)),
        out_shape=jax.ShapeDtypeStruct((m, n), BF16),
        compiler_params=_params("parallel", "parallel"),
        name="rope_matmul",
    )(x, w, cos_m, sin_m)


def _win_attn_body(q_ref, kp_ref, kc_ref, kn_ref, vp_ref, vc_ref, vn_ref, sink_ref, o_ref, *, rep, hd, nb):
    n = pl.program_id(2)
    blk = q_ref.shape[1]
    kps = kc_ref.shape[2] // hd
    rows = rep * blk
    ri = lax.broadcasted_iota(jnp.int32, (rows, 3 * blk), 0)
    qi = ri % blk + blk
    kj = lax.broadcasted_iota(jnp.int32, (rows, 3 * blk), 1)
    valid = jnp.abs(qi - kj) <= blk
    valid = valid & ((kj >= blk) | (n > 0)) & ((kj < 2 * blk) | (n < nb - 1))
    head_of_row = lax.broadcasted_iota(jnp.int32, (rows, 1), 0) // blk
    for kh in range(kps):
        cols = slice(kh * hd, (kh + 1) * hd)
        kcat = jnp.concatenate([kp_ref[0, :, cols], kc_ref[0, :, cols], kn_ref[0, :, cols]], axis=0)
        vcat = jnp.concatenate([vp_ref[0, :, cols], vc_ref[0, :, cols], vn_ref[0, :, cols]], axis=0)
        sink = jnp.zeros((rows, 1), F32)
        for r in range(rep):
            sink = jnp.where(head_of_row == r, sink_ref[kh, :, r:r + 1], sink)
        q0 = kh * rep * hd
        q = jnp.concatenate([q_ref[0, :, q0 + r * hd:q0 + (r + 1) * hd] for r in range(rep)], axis=0)
        s = lax.dot_general(q, kcat, (((1,), (1,)), ((), ())), preferred_element_type=F32) * (hd ** -0.5)
        s = jnp.where(valid, s, -jnp.inf)
        mx = jnp.maximum(jnp.max(s, axis=-1, keepdims=True), sink)
        p = jnp.exp(s - mx)
        denom = jnp.sum(p, axis=-1, keepdims=True) + jnp.exp(sink - mx)
        o = jnp.dot((p / denom).astype(BF16), vcat, preferred_element_type=F32)
        for r in range(rep):
            o_ref[0, :, q0 + r * hd:q0 + (r + 1) * hd] = o[r * blk:(r + 1) * blk].astype(o_ref.dtype)


AT_KV_HEADS_PER_STEP = 1


def _win_attn(qkv, qw, sink):
    b, s, _ = qkv.shape
    hd = AT_HEAD_DIM
    hkv = AT_KV_HEADS
    kps = AT_KV_HEADS_PER_STEP
    rep = qw // hd // hkv
    k_off = qw // (kps * hd)
    v_off = k_off + hkv // kps
    blk = AT_BLOCK
    nb = s // blk
    qspec = pl.BlockSpec((1, blk, kps * rep * hd), lambda i, g, n: (i, n, g))

    def kv(off, col):
        return pl.BlockSpec((1, blk, kps * hd), lambda i, g, n: (i, jnp.clip(n + off, 0, nb - 1), col + g))

    return pl.pallas_call(
        functools.partial(_win_attn_body, rep=rep, hd=hd, nb=nb),
        grid=(b, hkv // kps, nb),
        in_specs=[qspec, kv(-1, k_off), kv(0, k_off), kv(1, k_off), kv(-1, v_off), kv(0, v_off), kv(1, v_off),
                  pl.BlockSpec((kps, 1, rep), lambda i, g, n: (g, 0, 0))],
        out_specs=qspec,
        out_shape=jax.ShapeDtypeStruct((b, s, qw), BF16),
        compiler_params=_params("parallel", "parallel", "parallel"),
        name="win_attn",
    )(qkv, qkv, qkv, qkv, qkv, qkv, qkv, sink.astype(F32).reshape(hkv, 1, rep))


def _window_gqa_mixer(h, hn, bsz, seq, w_qkv, sink, w_out):
    m, d = hn.shape
    qw = d
    kvw = AT_KV_HEADS * AT_HEAD_DIM
    half = AT_HEAD_DIM // 2
    inv = ROPE_THETA ** (-jnp.arange(half, dtype=F32) / half)
    ang = jnp.arange(seq, dtype=F32)[:, None] * inv[None, :]
    cos_m = jnp.tile(jnp.concatenate([jnp.cos(ang), jnp.cos(ang)], axis=1), (bsz, 1))
    sin_m = jnp.tile(jnp.concatenate([-jnp.sin(ang), jnp.sin(ang)], axis=1), (bsz, 1))
    qkv = _rope_matmul(hn, w_qkv, cos_m, sin_m, qw + kvw).reshape(bsz, seq, qw + 2 * kvw)
    o = _win_attn(qkv, qw, sink)
    return _matmul(o.reshape(m, qw), w_out, F32, residual=h)


def _cross_attn_body(h_ref, g_ref, wq_ref, kv_ref, wo_ref, o_ref, *, heads, hd):
    hres = h_ref[0]
    ms = jnp.mean(hres * hres, axis=-1, keepdims=True)
    xn = (hres * lax.rsqrt(ms + RMS_EPS) * g_ref[...]).astype(BF16)
    q = jnp.dot(xn, wq_ref[...], preferred_element_type=F32).astype(BF16)
    kv = kv_ref[0]
    tm = hres.shape[0]
    s = jnp.concatenate(
        [lax.dot_general(q[:, a * hd:(a + 1) * hd], kv[:, a * hd:(a + 1) * hd], (((1,), (1,)), ((), ())),
                         preferred_element_type=F32) for a in range(heads)], axis=0) * (hd ** -0.5)
    mx = jnp.max(s, axis=-1, keepdims=True)
    p = jnp.exp(s - mx)
    p = (p / jnp.sum(p, axis=-1, keepdims=True)).astype(BF16)
    o = jnp.concatenate(
        [jnp.dot(p[a * tm:(a + 1) * tm], kv[:, (heads + a) * hd:(heads + a + 1) * hd],
                 preferred_element_type=F32) for a in range(heads)], axis=1).astype(BF16)
    o_ref[0] = hres + jnp.dot(o, wo_ref[...], preferred_element_type=F32)


def _cross_attention(h, mem, bsz, seq, norm_x, norm_m, w_q, w_kv, w_out):
    m, d = h.shape
    mt = mem.shape[1]
    cw = w_q.shape[1]
    memn = _rmsnorm(mem.reshape(bsz * mt, d), norm_m, BF16)
    kv = _matmul(memn, w_kv.astype(BF16), BF16).reshape(bsz, mt, 2 * cw)
    tm = _tile(seq, 256)
    out = pl.pallas_call(
        functools.partial(_cross_attn_body, heads=CA_HEADS, hd=CA_HEAD_DIM),
        grid=(bsz, seq // tm),
        in_specs=[pl.BlockSpec((1, tm, d), lambda i, j: (i, j, 0)),
                  pl.BlockSpec((1, d), lambda i, j: (0, 0)),
                  pl.BlockSpec((d, cw), lambda i, j: (0, 0)),
                  pl.BlockSpec((1, mt, 2 * cw), lambda i, j: (i, 0, 0)),
                  pl.BlockSpec((cw, d), lambda i, j: (0, 0))],
        out_specs=pl.BlockSpec((1, tm, d), lambda i, j: (i, j, 0)),
        out_shape=jax.ShapeDtypeStruct((bsz, seq, d), F32),
        compiler_params=_params("parallel", "parallel"),
        name="cross_attn",
    )(h.reshape(bsz, seq, d), norm_x.reshape(1, d).astype(F32), w_q.astype(BF16), kv, w_out.astype(BF16))
    return out.reshape(m, d)


MOE_ROW_BLOCK = 512
MOE_HIDDEN_TILE = 512
MOE_TOKEN_TILE = 256


HIGH_HALF = -65536
DMA_LOOP_UNROLL = 8


def _pack_pairs(x):
    half = x.shape[1] // 2
    bits = lax.bitcast_convert_type(x.astype(BF16).astype(F32), jnp.int32)
    return lax.shift_right_logical(bits[:, :half], jnp.int32(16)) | bits[:, half:]


def _unpack_pairs(p):
    lo = lax.bitcast_convert_type(lax.shift_left(p, jnp.int32(16)), F32)
    hi = lax.bitcast_convert_type(p & jnp.int32(HIGH_HALF), F32)
    return lo, hi


def _moe_norm_logits_body(h_ref, g_ref, wr_ref, xp_ref, lg_ref):
    x = h_ref[...]
    ms = jnp.mean(x * x, axis=-1, keepdims=True)
    xn = x * lax.rsqrt(ms + RMS_EPS) * g_ref[...]
    xp_ref[...] = _pack_pairs(xn)
    lg_ref[...] = jnp.dot(xn, wr_ref[...], preferred_element_type=F32, precision=lax.Precision.HIGHEST)


def _moe_norm_logits(h, g, w_route):
    m, d = h.shape
    tm = _tile(m, 256)
    return pl.pallas_call(
        _moe_norm_logits_body,
        grid=(m // tm,),
        in_specs=[pl.BlockSpec((tm, d), lambda i: (i, 0)),
                  pl.BlockSpec((1, d), lambda i: (0, 0)),
                  pl.BlockSpec((d, LANES), lambda i: (0, 0))],
        out_specs=[pl.BlockSpec((tm, d // 2), lambda i: (i, 0)), pl.BlockSpec((tm, LANES), lambda i: (i, 0))],
        out_shape=[jax.ShapeDtypeStruct((m, d // 2), jnp.int32), jax.ShapeDtypeStruct((m, LANES), F32)],
        compiler_params=_params("parallel"),
        name="moe_norm_logits",
    )(h, g.reshape(1, d).astype(F32), w_route)


def _row_copy(src_hbm, dst_ref, sem, src_row, dst_row):
    return pltpu.make_async_copy(src_hbm.at[pl.ds(src_row, 1)], dst_ref.at[pl.ds(dst_row, 1)], sem)


def _wait_rows(src_hbm, dst_ref, sem):
    pltpu.make_async_copy(src_hbm.at[pl.ds(0, dst_ref.shape[0])], dst_ref, sem).wait()


def _gather_rows_body(idx_ref, src_hbm, o_ref, sem, *, tm):
    base = pl.program_id(0) * tm

    def start(r8, carry):
        for u in range(DMA_LOOP_UNROLL):
            r = r8 * DMA_LOOP_UNROLL + u
            _row_copy(src_hbm, o_ref, sem, idx_ref[base + r], r).start(priority=u % 2)
        return carry

    lax.fori_loop(0, tm // DMA_LOOP_UNROLL, start, 0)
    _wait_rows(src_hbm, o_ref, sem)


def _gather_rows(src, idx):
    rows = idx.shape[0]
    d = src.shape[1]
    tm = _tile(rows, 512)
    return pl.pallas_call(
        functools.partial(_gather_rows_body, tm=tm),
        grid_spec=pltpu.PrefetchScalarGridSpec(
            num_scalar_prefetch=1,
            grid=(rows // tm,),
            in_specs=[pl.BlockSpec(memory_space=pl.ANY)],
            out_specs=pl.BlockSpec((tm, d), lambda i, idx_ref: (i, 0)),
            scratch_shapes=[pltpu.SemaphoreType.DMA(())]),
        out_shape=jax.ShapeDtypeStruct((rows, d), src.dtype),
        compiler_params=_params("arbitrary"),
        name="moe_gather",
    )(idx, src)


def _experts_body(be_ref, nused_ref, x_ref, wg_ref, wu_ref, wd_ref, o_ref, *acc, nj):
    i = pl.program_id(0)
    j = pl.program_id(1)
    used = i < nused_ref[0]

    @pl.when(jnp.logical_and(j == nj - 1, jnp.logical_not(used)))
    def _():
        o_ref[...] = jnp.zeros_like(o_ref)

    @pl.when(used)
    def _():
        lo, hi = _unpack_pairs(x_ref[...])
        x = jnp.concatenate([lo.astype(BF16), hi.astype(BF16)], axis=1)
        hid = (_silu(jnp.dot(x, wg_ref[0, 0], preferred_element_type=F32))
               * jnp.dot(x, wu_ref[0, 0], preferred_element_type=F32)).astype(BF16)
        part = jnp.dot(hid, wd_ref[0, 0], preferred_element_type=F32)
        if nj == 1:
            o_ref[...] = _pack_pairs(part)
            return
        acc_ref, = acc

        @pl.when(j == 0)
        def _():
            acc_ref[...] = part

        @pl.when(jnp.logical_and(j > 0, j < nj - 1))
        def _():
            acc_ref[...] += part

        @pl.when(j == nj - 1)
        def _():
            o_ref[...] = _pack_pairs(acc_ref[...] + part)


def _experts(buf, blk_expert, n_used, w_gate, w_up, w_down):
    rows, dh = buf.shape
    d = 2 * dh
    (w_gate, layer), (w_up, _), (w_down, _) = _layered(w_gate), _layered(w_up), _layered(w_down)
    hidden = w_gate.shape[3]
    tm = MOE_ROW_BLOCK
    th = _tile(hidden, MOE_HIDDEN_TILE)
    nj = hidden // th
    nblk = rows // tm

    def xmap(i, j, be, nu):
        return (jnp.minimum(i, nu[0] - 1), 0)

    return pl.pallas_call(
        functools.partial(_experts_body, nj=nj),
        grid_spec=pltpu.PrefetchScalarGridSpec(
            num_scalar_prefetch=2,
            grid=(nblk, nj),
            in_specs=[pl.BlockSpec((tm, dh), xmap),
                      pl.BlockSpec((1, 1, d, th), lambda i, j, be, nu: (layer, be[i], 0, j)),
                      pl.BlockSpec((1, 1, d, th), lambda i, j, be, nu: (layer, be[i], 0, j)),
                      pl.BlockSpec((1, 1, th, d), lambda i, j, be, nu: (layer, be[i], j, 0))],
            out_specs=pl.BlockSpec((tm, dh), lambda i, j, be, nu: (i, 0)),
            scratch_shapes=[pltpu.VMEM((tm, d), F32)] if nj > 1 else []),
        out_shape=jax.ShapeDtypeStruct((rows, dh), jnp.int32),
        compiler_params=_params("arbitrary", "arbitrary"),
        name="moe_experts",
    )(blk_expert, n_used, buf, w_gate, w_up, w_down)


def _combine_body(d0_ref, d1_ref, src_hbm, h_ref, gate_ref, o_ref, rows0, rows1, sem, *, tm):
    base = pl.program_id(0) * tm

    def start(r8, carry):
        for u in range(DMA_LOOP_UNROLL):
            r = r8 * DMA_LOOP_UNROLL + u
            _row_copy(src_hbm, rows0, sem.at[0], d0_ref[base + r], r).start(priority=0)
            _row_copy(src_hbm, rows1, sem.at[1], d1_ref[base + r], r).start(priority=1)
        return carry

    lax.fori_loop(0, tm // DMA_LOOP_UNROLL, start, 0)
    _wait_rows(src_hbm, rows0, sem.at[0])
    _wait_rows(src_hbm, rows1, sem.at[1])
    g = gate_ref[...]
    half = rows0.shape[1]
    a_lo, a_hi = _unpack_pairs(rows0[...])
    b_lo, b_hi = _unpack_pairs(rows1[...])
    o_ref[:, :half] = h_ref[:, :half] + a_lo * g[:, 0:1] + b_lo * g[:, 1:2]
    o_ref[:, half:] = h_ref[:, half:] + a_hi * g[:, 0:1] + b_hi * g[:, 1:2]


def _combine(h, out_buf, dest, gates):
    m, d = h.shape
    tm = _tile(m, MOE_TOKEN_TILE)
    return pl.pallas_call(
        functools.partial(_combine_body, tm=tm),
        grid_spec=pltpu.PrefetchScalarGridSpec(
            num_scalar_prefetch=2,
            grid=(m // tm,),
            in_specs=[pl.BlockSpec(memory_space=pl.ANY),
                      pl.BlockSpec((tm, d), lambda i, a, b: (i, 0)),
                      pl.BlockSpec((tm, MOE_TOP_K), lambda i, a, b: (i, 0))],
            out_specs=pl.BlockSpec((tm, d), lambda i, a, b: (i, 0)),
            scratch_shapes=[pltpu.VMEM((tm, d // 2), jnp.int32), pltpu.VMEM((tm, d // 2), jnp.int32),
                            pltpu.SemaphoreType.DMA((2,))]),
        out_shape=jax.ShapeDtypeStruct((m, d), F32),
        compiler_params=_params("arbitrary"),
        name="moe_combine",
    )(dest[:, 0], dest[:, 1], out_buf, h, gates)


def _hier_moe(h, norm_g, w_group, b_group, w_expert, b_expert, w_gate, w_up, w_down):
    n, d = h.shape
    w_route = jnp.concatenate([w_group, w_expert], axis=1).astype(F32)
    w_route = jnp.pad(w_route, ((0, 0), (0, LANES - w_route.shape[1])))
    xp, logits = _moe_norm_logits(h, norm_g, w_route)
    g_logits = logits[:, :MOE_GROUPS] + b_group.astype(F32)
    g_prob = jax.nn.softmax(g_logits, axis=-1)
    g_sel = jnp.argmax(g_logits, axis=-1)
    g_weight = jnp.take_along_axis(g_prob, g_sel[:, None], axis=-1)
    e_logits = (logits[:, MOE_GROUPS:MOE_GROUPS + MOE_EXPERTS] + b_expert.astype(F32)
                ).reshape(n, MOE_GROUPS, MOE_EXPERTS_PER_GROUP)
    e_logits = jnp.take_along_axis(e_logits, g_sel[:, None, None], axis=1)[:, 0]
    top_val, top_idx = lax.top_k(e_logits, MOE_TOP_K)
    gates = g_weight * jax.nn.softmax(top_val, axis=-1)
    experts = (g_sel[:, None] * MOE_EXPERTS_PER_GROUP + top_idx).astype(jnp.int32)

    nk = n * MOE_TOP_K
    tm = MOE_ROW_BLOCK
    e_flat = experts.reshape(nk)
    onehot = (e_flat[:, None] == jnp.arange(MOE_EXPERTS, dtype=jnp.int32)[None, :]).astype(jnp.int32)
    rank = jnp.sum((jnp.cumsum(onehot, axis=0) - onehot) * onehot, axis=1)
    counts = jnp.sum(onehot, axis=0)
    padded = (counts + tm - 1) // tm * tm
    pend = jnp.cumsum(padded)
    pstart = pend - padded
    dest = (pstart[e_flat] + rank).astype(jnp.int32)
    n_blocks = -(-nk // tm) + MOE_EXPERTS
    rows = n_blocks * tm
    src = jnp.zeros((rows,), jnp.int32).at[dest].set(jnp.arange(nk, dtype=jnp.int32) // MOE_TOP_K)
    blk_expert = jnp.minimum(jnp.searchsorted(pend, jnp.arange(n_blocks, dtype=jnp.int32) * tm, side='right'),
                             MOE_EXPERTS - 1).astype(jnp.int32)
    n_used = (pend[-1] // tm).astype(jnp.int32).reshape(1)

    buf = _gather_rows(xp, src)
    out_buf = _experts(buf, blk_expert, n_used, w_gate, w_up, w_down)
    return _combine(h, out_buf, dest.reshape(n, MOE_TOP_K), gates.astype(F32))


def _trunk(x, mem, p):
    bsz, seq, d = x.shape
    m = bsz * seq
    depth = p['norm_mix'].shape[0]
    lb_soft = jax.nn.softmax(p['hg_lb'].astype(F32), axis=0)
    lower_bounds = jnp.cumsum(lb_soft, axis=0) - lb_soft[0]
    h = x.reshape(m, d)
    for i in range(depth):
        kind, j = i % N_MIXERS, i // N_MIXERS
        hn = _rmsnorm(h, p['norm_mix'][i], BF16)
        if kind == 0:
            h = _ssd_mixer(h, hn, bsz, seq, (p['ssd_w_in'], j), p['ssd_conv_w'][j], p['ssd_conv_b'][j],
                           p['ssd_dt_bias'][j], p['ssd_a_log'][j], p['ssd_d'][j], p['ssd_norm'][j],
                           (p['ssd_w_out'], j))
        elif kind == 1:
            h = _hgrn2_mixer(h, hn, bsz, seq, (p['hg_w_in'], j), lower_bounds[i], p['hg_norm'][j],
                             (p['hg_w_out'], j))
        else:
            h = _window_gqa_mixer(h, hn, bsz, seq, (p['at_w_qkv'], j), p['at_sink'][j], (p['at_w_out'], j))
        h = _cross_attention(h, mem, bsz, seq, p['norm_cross'][i], p['norm_mem'][i],
                             p['ca_w_q'][i], p['ca_w_kv'][i], p['ca_w_out'][i])
        h = _hier_moe(h, p['norm_moe'][i], p['moe_w_group'][i], p['moe_b_group'][i], p['moe_w_expert'][i],
                      p['moe_b_expert'][i], (p['moe_w_gate'], i), (p['moe_w_up'], i), (p['moe_w_down'], i))
    return _rmsnorm(h, p['norm_final'], F32).reshape(bsz, seq, d)


def kernel(x_prompt, x_sample, mem_prompt, mem_sample, norm_mix, norm_cross, norm_mem, norm_moe, norm_final,
           ssd_w_in, ssd_conv_w, ssd_conv_b, ssd_dt_bias, ssd_a_log, ssd_d, ssd_norm, ssd_w_out,
           hg_w_in, hg_lb, hg_norm, hg_w_out, at_w_qkv, at_sink, at_w_out, ca_w_q, ca_w_kv, ca_w_out,
           moe_w_group, moe_b_group, moe_w_expert, moe_b_expert, moe_w_gate, moe_w_up, moe_w_down):
    p = dict(norm_mix=norm_mix, norm_cross=norm_cross, norm_mem=norm_mem, norm_moe=norm_moe,
             norm_final=norm_final, ssd_w_in=ssd_w_in, ssd_conv_w=ssd_conv_w, ssd_conv_b=ssd_conv_b,
             ssd_dt_bias=ssd_dt_bias, ssd_a_log=ssd_a_log, ssd_d=ssd_d, ssd_norm=ssd_norm, ssd_w_out=ssd_w_out,
             hg_w_in=hg_w_in, hg_lb=hg_lb, hg_norm=hg_norm, hg_w_out=hg_w_out,
             at_w_qkv=at_w_qkv, at_sink=at_sink, at_w_out=at_w_out,
             ca_w_q=ca_w_q, ca_w_kv=ca_w_kv, ca_w_out=ca_w_out,
             moe_w_group=moe_w_group, moe_b_group=moe_b_group, moe_w_expert=moe_w_expert,
             moe_b_expert=moe_b_expert, moe_w_gate=moe_w_gate, moe_w_up=moe_w_up, moe_w_down=moe_w_down)
    for name in ('ssd_w_in', 'ssd_w_out', 'hg_w_in', 'hg_w_out', 'at_w_qkv', 'at_w_out', 'ca_w_q', 'ca_w_kv',
                 'ca_w_out', 'moe_w_gate', 'moe_w_up', 'moe_w_down'):
        p[name] = p[name].astype(BF16)
    return (_trunk(x_prompt, mem_prompt, p), _trunk(x_sample, mem_sample, p))
```
